```python
import math
import jax
import jax.numpy as jnp
from jax import lax
import numpy as np


D_MODEL = 1024
BATCH = 4
SEQ = 4096
DEPTH = 2
DEC_BATCH = 2
DEC_SEQ = 16384
PAST_LEN = 128

GRID_W = 64
PLE_DIM = 256
D_FF = 2816
QBLK = 128
NORM_EPS = 1e-6
ROPE_THETA = 10000.0
NEG_INF = -1e30

HA = 8
Q_RANK = 256
KV_RANK = 128
NOPE_A = 64
ROPE_A = 32
V_A = 64
QK_A = NOPE_A + ROPE_A

HD = 64
DIL_PAIRS = ((128, 1), (512, 4), (2048, 16))
N_GROUPS_B = 3
HPG_B = 4
N_HEADS_B = N_GROUPS_B * HPG_B
T5_BUCKETS = 32
T5_MAX_DIST = 1024

HC = 8
KVC = 2
GC = HC // KVC

N_BRANCH = 3
A_COLS = Q_RANK + KV_RANK + ROPE_A
B_COLS = 3 * N_HEADS_B * HD
C_COLS = (HC + 2 * KVC) * HD
IN_COLS = A_COLS + B_COLS + C_COLS

kernel_name = 'hybrid_gated_mla_dilated_axialgqa_encoder'


def rmsnorm(x, g):
    xf = x.astype(jnp.float32)
    y = xf * lax.rsqrt(jnp.mean(xf * xf, axis=-1, keepdims=True) + NORM_EPS)
    return (y * g.astype(jnp.float32)).astype(x.dtype)


def swiglu(x, w_in, w_out):
    a, b = jnp.split(x @ w_in, 2, axis=-1)
    return (jax.nn.silu(a) * b) @ w_out


def rope(x, pos):
    half = x.shape[-1] // 2
    freqs = (ROPE_THETA ** (-np.arange(half) / half)).astype(np.float32)
    ang = pos.astype(jnp.float32)[:, None] * jnp.asarray(freqs)[None, :]
    cos = jnp.cos(ang)[:, None, :].astype(x.dtype)
    sin = jnp.sin(ang)[:, None, :].astype(x.dtype)
    x1, x2 = x[..., :half], x[..., half:]
    return jnp.concatenate([x1 * cos - x2 * sin, x1 * sin + x2 * cos], axis=-1)


def dense_attn(q, k, v, scale):
    B, S, Hk, G, dk = q.shape
    nq = S // QBLK
    qb = q.reshape(B, nq, QBLK, Hk, G, dk).transpose(1, 0, 2, 3, 4, 5)

    def one_block(qi):
        s = jnp.einsum('bqhgd,bkhd->bhgqk', qi, k).astype(jnp.float32) * scale
        p = jax.nn.softmax(s, axis=-1).astype(v.dtype)
        return jnp.einsum('bhgqk,bkhd->bqhgd', p, v)

    o = lax.map(one_block, qb)
    return o.transpose(1, 0, 2, 3, 4, 5).reshape(B, S, Hk, G, v.shape[-1])


def t5_bucket(rel):
    nb = T5_BUCKETS // 2
    max_exact = nb // 2
    n = np.abs(rel)
    large = max_exact + (np.log(np.maximum(n, 1) / max_exact) / math.log(T5_MAX_DIST / max_exact) * (nb - max_exact)).astype(np.int32)
    large = np.minimum(large, nb - 1)
    return (rel > 0).astype(np.int32) * nb + np.where(n < max_exact, n, large).astype(np.int32)


def local_attn(q, k, v, bias, R):
    N, L, H, dh = q.shape
    nb = -(-L // R)
    Lp = nb * R
    qp = jnp.pad(q, ((0, 0), (0, Lp - L), (0, 0), (0, 0))).reshape(N, nb, R, H, dh)
    kpad = ((0, 0), (R, Lp - L + R), (0, 0), (0, 0))
    kb = jnp.pad(k, kpad).reshape(N, nb + 2, R, H, dh)
    vb = jnp.pad(v, kpad).reshape(N, nb + 2, R, H, dh)
    kw = jnp.concatenate([kb[:, :-2], kb[:, 1:-1], kb[:, 2:]], axis=2)
    vw = jnp.concatenate([vb[:, :-2], vb[:, 1:-1], vb[:, 2:]], axis=2)
    qpos = np.arange(Lp).reshape(nb, R)[:, :, None]
    kpos = (np.arange(nb)[:, None] * R - R + np.arange(3 * R)[None, :])[:, None, :]
    valid = (np.abs(kpos - qpos) <= R) & (kpos >= 0) & (kpos < L)
    s = jnp.einsum('nbqhd,nbkhd->nbhqk', qp, kw).astype(jnp.float32) * (dh ** -0.5) + bias
    s = jnp.where(jnp.asarray(valid)[None, :, None], s, NEG_INF)
    lse = jax.nn.logsumexp(s, axis=-1)
    p = jnp.exp(s - lse[..., None]).astype(v.dtype)
    o = jnp.einsum('nbhqk,nbkhd->nbqhd', p, vw).reshape(N, Lp, H, dh)[:, :L]
    lse = lse.transpose(0, 1, 3, 2).reshape(N, Lp, H)[:, :L]
    return o, lse


def dilated_group(q, k, v, tab, window, dil):
    B, S, H, dh = q.shape
    R = window // (2 * dil)
    L = S // dil

    def split(t):
        return t.reshape(B, L, dil, H, dh).transpose(0, 2, 1, 3, 4).reshape(B * dil, L, H, dh)

    qi = np.arange(R)[:, None]
    kj = np.arange(3 * R)[None, :]
    bucket = t5_bucket((kj - R - qi) * dil)
    bias = jnp.transpose(tab[bucket], (2, 0, 1)).astype(jnp.float32)
    o, lse = local_attn(split(q), split(k), split(v), bias, R)
    o = o.reshape(B, dil, L, H, dh).transpose(0, 2, 1, 3, 4).reshape(B, S, H, dh)
    lse = lse.reshape(B, dil, L, H).transpose(0, 2, 1, 3).reshape(B, S, H)
    return o, lse


def mla_mixer(a_in, pos, g_cq, g_ckv, w_uq, w_ukv, g_q, g_k):
    B, S, _ = a_in.shape
    c_q = rmsnorm(a_in[..., :Q_RANK], g_cq)
    c_kv = rmsnorm(a_in[..., Q_RANK:Q_RANK + KV_RANK], g_ckv)
    k_rope = a_in[..., Q_RANK + KV_RANK:][:, :, None, :]
    q = (c_q @ w_uq).reshape(B, S, HA, QK_A)
    kv = (c_kv @ w_ukv).reshape(B, S, HA, NOPE_A + V_A)
    q_nope = rmsnorm(q[..., :NOPE_A], g_q[:NOPE_A])
    q_rope = rope(rmsnorm(q[..., NOPE_A:], g_q[NOPE_A:]), pos)
    k_nope = rmsnorm(kv[..., :NOPE_A], g_k[:NOPE_A])
    k_rope = rope(rmsnorm(k_rope, g_k[NOPE_A:]), pos)
    q = jnp.concatenate([q_nope, q_rope], axis=-1)
    k = jnp.concatenate([k_nope, jnp.broadcast_to(k_rope, (B, S, HA, ROPE_A))], axis=-1)
    v = kv[..., NOPE_A:]
    o = dense_attn(q[:, :, :, None, :], k, v, QK_A ** -0.5)
    return o.reshape(B, S, HA * V_A)


def dilated_mixer(b_in, g_q, g_k, rel_bias):
    B, S, _ = b_in.shape
    qkv = b_in.reshape(B, S, 3, N_GROUPS_B, HPG_B, HD)
    q = rmsnorm(qkv[:, :, 0], g_q)
    k = rmsnorm(qkv[:, :, 1], g_k)
    v = qkv[:, :, 2]
    outs = []
    lses = []
    for g, (window, dil) in enumerate(DIL_PAIRS):
        tab = rel_bias[:, g * HPG_B:(g + 1) * HPG_B]
        o, l = dilated_group(q[:, :, g], k[:, :, g], v[:, :, g], tab, window, dil)
        outs.append(o)
        lses.append(l)
    outs = jnp.stack(outs, axis=0)
    lses = jnp.stack(lses, axis=0)
    alpha = jax.nn.softmax(lses, axis=0).astype(outs.dtype)
    o = jnp.sum(alpha[..., None] * outs, axis=0)
    return o.reshape(B, S, HPG_B * HD)


def axial_rope(t, row_pos, col_pos):
    h = t.shape[-1] // 2
    return jnp.concatenate([rope(t[..., :h], row_pos), rope(t[..., h:], col_pos)], axis=-1)


def axial_gqa_mixer(c_in, row_pos, col_pos, g_q, g_k):
    B, S, _ = c_in.shape
    q = c_in[..., :HC * HD].reshape(B, S, HC, HD)
    k = c_in[..., HC * HD:(HC + KVC) * HD].reshape(B, S, KVC, HD)
    v = c_in[..., (HC + KVC) * HD:].reshape(B, S, KVC, HD)
    q = axial_rope(rmsnorm(q, g_q), row_pos, col_pos)
    k = axial_rope(rmsnorm(k, g_k), row_pos, col_pos)
    o = dense_attn(q.reshape(B, S, KVC, GC, HD), k, v, HD ** -0.5)
    return o.reshape(B, S, HC * HD)


def _layer(x, pe, W, i, pos, row_pos, col_pos):
    B, S, _ = x.shape
    x = x + 0.5 * swiglu(rmsnorm(x, W['g_ffn1'][i]), W['w_ffn1_in'][i], W['w_ffn1_out'][i])
    u = rmsnorm(x, W['g_mix'][i])
    proj = u @ W['w_in'][i]
    a_in = proj[..., :A_COLS]
    b_in = proj[..., A_COLS:A_COLS + B_COLS]
    c_in = proj[..., A_COLS + B_COLS:]
    o_a = mla_mixer(a_in, pos, W['g_cq'][i], W['g_ckv'][i], W['w_uq'][i], W['w_ukv'][i], W['g_qa'][i], W['g_ka'][i])
    o_b = dilated_mixer(b_in, W['g_qb'][i], W['g_kb'][i], W['rel_bias'])
    o_c = axial_gqa_mixer(c_in, row_pos, col_pos, W['g_qc'][i], W['g_kc'][i])
    gates = jax.nn.sigmoid(u @ W['w_gate'][i] + W['b_gate'][i]).reshape(B, S, N_BRANCH, D_MODEL)
    merged = (gates[:, :, 0] * (o_a @ W['w_oa'][i])
              + gates[:, :, 1] * (o_b @ W['w_ob'][i])
              + gates[:, :, 2] * (o_c @ W['w_oc'][i]))
    x = x + merged @ W['w_out'][i]
    x = x + 0.5 * swiglu(rmsnorm(x, W['g_ffn2'][i]), W['w_ffn2_in'][i], W['w_ffn2_out'][i])
    g = jax.nn.sigmoid(rmsnorm(x, W['g_ple'][i]) @ W['w_pg'][i])
    return x + g * (pe @ W['w_ple'][i])


def _trunk(x, p, W):
    S = x.shape[1]
    rows = S // GRID_W
    pos = jnp.arange(S, dtype=jnp.int32)
    row_pos = jnp.repeat(jnp.arange(rows, dtype=jnp.int32), GRID_W)
    col_pos = pos % GRID_W
    for i in range(DEPTH):
        x = _layer(x, p[i], W, i, pos, row_pos, col_pos)
    return x


def setup_inputs(seed: int = 0) -> dict:
    key = jax.random.key(seed)
    keys = jax.random.split(key, 32)
    cnt = [0]

    def nk():
        k = keys[cnt[0]]
        cnt[0] += 1
        return k

    def nrm(shape, fan_in):
        return jax.random.normal(nk(), shape, jnp.float32) * (fan_in ** -0.5)

    def gain(shape):
        return 1.0 + 0.05 * jax.random.normal(nk(), shape, jnp.float32)

    L = DEPTH
    return {
        'x_prompt': jax.random.normal(nk(), (BATCH, SEQ, D_MODEL), jnp.float32),
        'x_sample': jax.random.normal(nk(), (DEC_BATCH, DEC_SEQ, D_MODEL), jnp.float32),
        'p_prompt': jax.random.normal(nk(), (DEPTH, BATCH, SEQ, PLE_DIM), jnp.float32),
        'p_sample': jax.random.normal(nk(), (DEPTH, DEC_BATCH, DEC_SEQ, PLE_DIM), jnp.float32),
        'g_ffn1': gain((L, D_MODEL)),
        'w_ffn1_in': nrm((L, D_MODEL, 2 * D_FF), D_MODEL),
        'w_ffn1_out': nrm((L, D_FF, D_MODEL), D_FF),
        'g_mix': gain((L, D_MODEL)),
        'w_in': nrm((L, D_MODEL, IN_COLS), D_MODEL),
        'g_cq': gain((L, Q_RANK)),
        'g_ckv': gain((L, KV_RANK)),
        'w_uq': nrm((L, Q_RANK, HA * QK_A), Q_RANK),
        'w_ukv': nrm((L, KV_RANK, HA * (NOPE_A + V_A)), KV_RANK),
        'g_qa': gain((L, QK_A)),
        'g_ka': gain((L, QK_A)),
        'g_qb': gain((L, HD)),
        'g_kb': gain((L, HD)),
        'rel_bias': 0.5 * jax.random.normal(nk(), (T5_BUCKETS, N_HEADS_B), jnp.float32),
        'g_qc': gain((L, HD)),
        'g_kc': gain((L, HD)),
        'w_gate': nrm((L, D_MODEL, N_BRANCH * D_MODEL), D_MODEL),
        'b_gate': 0.1 * jax.random.normal(nk(), (L, N_BRANCH * D_MODEL), jnp.float32),
        'w_oa': nrm((L, HA * V_A, D_MODEL), HA * V_A),
        'w_ob': nrm((L, HPG_B * HD, D_MODEL), HPG_B * HD),
        'w_oc': nrm((L, HC * HD, D_MODEL), HC * HD),
        'w_out': nrm((L, D_MODEL, D_MODEL), D_MODEL),
        'g_ffn2': gain((L, D_MODEL)),
        'w_ffn2_in': nrm((L, D_MODEL, 2 * D_FF), D_MODEL),
        'w_ffn2_out': nrm((L, D_FF, D_MODEL), D_FF),
        'g_ple': gain((L, D_MODEL)),
        'w_pg': nrm((L, D_MODEL, D_MODEL), D_MODEL),
        'w_ple': nrm((L, PLE_DIM, D_MODEL), PLE_DIM),
    }


def reference(x_prompt, x_sample, p_prompt, p_sample, g_ffn1, w_ffn1_in, w_ffn1_out, g_mix, w_in,
              g_cq, g_ckv, w_uq, w_ukv, g_qa, g_ka, g_qb, g_kb, rel_bias, g_qc, g_kc,
              w_gate, b_gate, w_oa, w_ob, w_oc, w_out, g_ffn2, w_ffn2_in, w_ffn2_out,
              g_ple, w_pg, w_ple):
    W = dict(g_ffn1=g_ffn1, w_ffn1_in=w_ffn1_in, w_ffn1_out=w_ffn1_out, g_mix=g_mix, w_in=w_in,
             g_cq=g_cq, g_ckv=g_ckv, w_uq=w_uq, w_ukv=w_ukv, g_qa=g_qa, g_ka=g_ka,
             g_qb=g_qb, g_kb=g_kb, rel_bias=rel_bias, g_qc=g_qc, g_kc=g_kc,
             w_gate=w_gate, b_gate=b_gate, w_oa=w_oa, w_ob=w_ob, w_oc=w_oc, w_out=w_out,
             g_ffn2=g_ffn2, w_ffn2_in=w_ffn2_in, w_ffn2_out=w_ffn2_out,
             g_ple=g_ple, w_pg=w_pg, w_ple=w_ple)
    y_prompt = _trunk(x_prompt, p_prompt, W)
    y_sample = _trunk(x_sample, p_sample, W)
    return (y_prompt, y_sample)
```

```python
import functools
import math

import numpy as np
import jax
import jax.numpy as jnp
from jax import lax
from jax.experimental import pallas as pl
from jax.experimental.pallas import tpu as pltpu

F32 = jnp.float32
BF16 = jnp.bfloat16

D_MODEL = 1024
GRID_W = 64
PLE_DIM = 256
D_FF = 2816
NORM_EPS = 1e-6
ROPE_THETA = 10000.0
NEG_INF = -1e30

HA = 8
Q_RANK = 256
KV_RANK = 128
NOPE_A = 64
ROPE_A = 32
V_A = 64
QK_A = NOPE_A + ROPE_A

HD = 64
DIL_PAIRS = ((128, 1), (512, 4), (2048, 16))
N_GROUPS_B = 3
HPG_B = 4
N_HEADS_B = N_GROUPS_B * HPG_B
T5_BUCKETS = 32
T5_MAX_DIST = 1024
BAND_R = 64

HC = 8
KVC = 2
GC = HC // KVC

A_COLS = Q_RANK + KV_RANK + ROPE_A
B_COLS = 3 * N_HEADS_B * HD
C_COLS = (HC + 2 * KVC) * HD
GROUP_B_COLS = HPG_B * HD

ROPE_HALF = 16
QK_PAD = 128
V_ROWS = 80
LOG2E = math.log2(math.e)

VMEM_LIMIT_BYTES = 56 * 1024 * 1024
FFN_CHUNK = 256


def _cparams(semantics):
    return pltpu.CompilerParams(dimension_semantics=semantics, vmem_limit_bytes=VMEM_LIMIT_BYTES)


def _const_spec(shape):
    nd = len(shape)
    return pl.BlockSpec(shape, lambda *_: (0,) * nd, pipeline_mode=pl.Buffered(1))


def _rms_rows(x, g):
    ms = jnp.mean(x * x, axis=-1, keepdims=True)
    return x * lax.rsqrt(ms + NORM_EPS) * g


def _rms_cols(x, g):
    ms = jnp.mean(x * x, axis=0, keepdims=True)
    return x * lax.rsqrt(ms + NORM_EPS) * g


def _rope_cols(x, cos, sin):
    x1, x2 = x[:ROPE_HALF], x[ROPE_HALF:]
    return x1 * cos - x2 * sin, x1 * sin + x2 * cos


def _ffn_kernel(*refs, n_chunks, ple):
    if ple:
        x_ref, g_ref, win_ref, wout_ref, pe_ref, gple_ref, wpg_ref, wple_ref, o_ref, acc_ref = refs
    else:
        x_ref, g_ref, win_ref, wout_ref, o_ref, acc_ref = refs
    x = x_ref[...]
    u = _rms_rows(x, g_ref[...]).astype(BF16)
    acc_ref[...] = jnp.zeros_like(acc_ref)

    def body(c, carry):
        a = jnp.dot(u, win_ref[c], preferred_element_type=F32)
        b = jnp.dot(u, win_ref[n_chunks + c], preferred_element_type=F32)
        h = (a * jax.nn.sigmoid(a) * b).astype(BF16)
        acc_ref[...] += jnp.dot(h, wout_ref[c], preferred_element_type=F32)
        return carry

    lax.fori_loop(0, n_chunks, body, 0)
    y = x + 0.5 * acc_ref[...]
    if ple:
        gate = jax.nn.sigmoid(
            jnp.dot(_rms_rows(y, gple_ref[...]).astype(BF16), wpg_ref[...], preferred_element_type=F32))
        y = y + gate * jnp.dot(pe_ref[...].astype(BF16), wple_ref[...], preferred_element_type=F32)
    o_ref[...] = y


def _ffn(x2d, g, w_in_c, w_out_c, ple_args=None, tm=512):
    n, d = x2d.shape
    n_chunks = w_out_c.shape[0]
    tm = min(tm, n)
    row = lambda i: (i, 0)
    in_specs = [pl.BlockSpec((tm, d), row), _const_spec(g.shape), _const_spec(w_in_c.shape), _const_spec(w_out_c.shape)]
    args = [x2d, g, w_in_c, w_out_c]
    if ple_args is not None:
        pe2d, g_ple, w_pg, w_ple = ple_args
        in_specs += [pl.BlockSpec((tm, pe2d.shape[1]), row), _const_spec(g_ple.shape), _const_spec(w_pg.shape),
                     _const_spec(w_ple.shape)]
        args += [pe2d, g_ple, w_pg, w_ple]
    return pl.pallas_call(
        functools.partial(_ffn_kernel, n_chunks=n_chunks, ple=ple_args is not None),
        grid=(n // tm,),
        in_specs=in_specs,
        out_specs=pl.BlockSpec((tm, d), row),
        out_shape=jax.ShapeDtypeStruct((n, d), F32),
        scratch_shapes=[pltpu.VMEM((tm, d), F32)],
        compiler_params=_cparams(("parallel",)),
        name="ffn_ple" if ple_args is not None else "ffn",
    )(*args)


_T_CQ = 0
_T_CKV = Q_RANK
_T_KR = Q_RANK + KV_RANK
_T_QC = A_COLS
_T_KC = A_COLS + HC * HD
_T_VC = A_COLS + (HC + KVC) * HD
_T_ROWS = A_COLS + C_COLS


def _proj_kernel(x_ref, gmix_ref, wb_ref, wact_ref, ones_ref, gqb_ref, gkb_ref, gcq_ref, gckv_ref, wuqt_ref,
                 wukvt_ref, gqa_ref, gka_ref, gqc_ref, gkc_ref, tab_ref,
                 qkvb_ref, qat_ref, ka_ref, vat_ref, qct_ref, kc_ref, vct_ref, *, scale_a, scale_c):
    ts = x_ref.shape[0]
    ub = _rms_rows(x_ref[...], gmix_ref[...]).astype(BF16)

    pb = jnp.dot(ub, wb_ref[...], preferred_element_type=F32)
    ones = ones_ref[...]
    n_qk = N_HEADS_B * HD
    for part, g_ref in ((0, gqb_ref), (1, gkb_ref)):
        for c in range(N_GROUPS_B):
            lo = part * n_qk + c * GROUP_B_COLS
            xx = pb[:, lo:lo + GROUP_B_COLS]
            ms = jnp.dot((xx * xx).astype(BF16), ones, preferred_element_type=F32)
            qkvb_ref[:, lo:lo + GROUP_B_COLS] = (xx * lax.rsqrt(ms + NORM_EPS) * g_ref[...]).astype(BF16)
    qkvb_ref[:, 2 * n_qk:] = pb[:, 2 * n_qk:].astype(BF16)

    pt = lax.dot_general(wact_ref[...], ub, (((1,), (1,)), ((), ())), preferred_element_type=F32)
    cos_p, sin_p, cos_r, sin_r, cos_c, sin_c = (tab_ref[i] for i in range(6))
    ones_rows = jnp.ones((V_ROWS - V_A, ts), BF16)

    cq = _rms_cols(pt[_T_CQ:_T_CQ + Q_RANK], gcq_ref[...]).astype(BF16)
    qa = jnp.dot(wuqt_ref[...], cq, preferred_element_type=F32)
    ckv = _rms_cols(pt[_T_CKV:_T_CKV + KV_RANK], gckv_ref[...]).astype(BF16)
    kv = jnp.dot(wukvt_ref[...], ckv, preferred_element_type=F32)
    gqa = gqa_ref[...]
    gka = gka_ref[...]
    kr1, kr2 = _rope_cols(_rms_cols(pt[_T_KR:_T_KR + ROPE_A], gka[NOPE_A:]), cos_p, sin_p)
    k_tail = jnp.zeros((QK_PAD - QK_A, ts), F32)
    for h in range(HA):
        q0 = h * QK_A
        qn = _rms_cols(qa[q0:q0 + NOPE_A], gqa[:NOPE_A])
        q1, q2 = _rope_cols(_rms_cols(qa[q0 + NOPE_A:q0 + QK_A], gqa[NOPE_A:]), cos_p, sin_p)
        qat_ref[h, 0:NOPE_A] = (qn * scale_a).astype(BF16)
        qat_ref[h, NOPE_A:NOPE_A + ROPE_HALF] = (q1 * scale_a).astype(BF16)
        qat_ref[h, NOPE_A + ROPE_HALF:QK_A] = (q2 * scale_a).astype(BF16)
        qat_ref[h, QK_A:QK_PAD] = jnp.zeros((QK_PAD - QK_A, ts), BF16)
        k0 = h * (NOPE_A + V_A)
        kn = _rms_cols(kv[k0:k0 + NOPE_A], gka[:NOPE_A])
        kt = jnp.concatenate([kn, kr1, kr2, k_tail], axis=0)
        ka_ref[h] = kt.T.astype(BF16)
        vat_ref[h, 0:V_A] = kv[k0 + NOPE_A:k0 + NOPE_A + V_A].astype(BF16)
        vat_ref[h, V_A:V_ROWS] = ones_rows

    def axial(t):
        a1, a2 = _rope_cols(t[:HD // 2], cos_r, sin_r)
        b1, b2 = _rope_cols(t[HD // 2:], cos_c, sin_c)
        return a1, a2, b1, b2

    gqc = gqc_ref[...]
    gkc = gkc_ref[...]
    for h in range(HC):
        parts = axial(_rms_cols(pt[_T_QC + h * HD:_T_QC + (h + 1) * HD], gqc) * scale_c)
        qct_ref[h, 0:HD] = jnp.concatenate(parts, axis=0).astype(BF16)
        qct_ref[h, HD:QK_PAD] = jnp.zeros((QK_PAD - HD, ts), BF16)
    c_tail = jnp.zeros((QK_PAD - HD, ts), F32)
    for h in range(KVC):
        parts = axial(_rms_cols(pt[_T_KC + h * HD:_T_KC + (h + 1) * HD], gkc))
        kt = jnp.concatenate(list(parts) + [c_tail], axis=0)
        kc_ref[h] = kt.T.astype(BF16)
        vct_ref[h, 0:HD] = pt[_T_VC + h * HD:_T_VC + (h + 1) * HD].astype(BF16)
        vct_ref[h, HD:V_ROWS] = ones_rows


def _proj(x3d, lw, tab, ts=512):
    b, s, d = x3d.shape
    ts = min(ts, s)
    consts = [lw["g_mix"], lw["w_b"], lw["w_act"], lw["ones_blk"], lw["g_qb"], lw["g_kb"], lw["g_cq"], lw["g_ckv"],
              lw["w_uqt"], lw["w_ukvt"], lw["g_qa"], lw["g_ka"], lw["g_qc"], lw["g_kc"]]
    in_specs = ([pl.BlockSpec((None, ts, d), lambda bi, i: (bi, i, 0))] + [_const_spec(c.shape) for c in consts]
                + [pl.BlockSpec((6, ROPE_HALF, ts), lambda bi, i: (0, 0, i))])
    tok_major = lambda h: pl.BlockSpec((None, h, ts, QK_PAD), lambda bi, i: (bi, 0, i, 0))
    feat_major = lambda h, r: pl.BlockSpec((None, h, r, ts), lambda bi, i: (bi, 0, 0, i))
    out_specs = [pl.BlockSpec((None, ts, B_COLS), lambda bi, i: (bi, i, 0)),
                 feat_major(HA, QK_PAD), tok_major(HA), feat_major(HA, V_ROWS),
                 feat_major(HC, QK_PAD), tok_major(KVC), feat_major(KVC, V_ROWS)]
    out_shape = [jax.ShapeDtypeStruct((b, s, B_COLS), BF16),
                 jax.ShapeDtypeStruct((b, HA, QK_PAD, s), BF16), jax.ShapeDtypeStruct((b, HA, s, QK_PAD), BF16),
                 jax.ShapeDtypeStruct((b, HA, V_ROWS, s), BF16),
                 jax.ShapeDtypeStruct((b, HC, QK_PAD, s), BF16), jax.ShapeDtypeStruct((b, KVC, s, QK_PAD), BF16),
                 jax.ShapeDtypeStruct((b, KVC, V_ROWS, s), BF16)]
    return pl.pallas_call(
        functools.partial(_proj_kernel, scale_a=QK_A ** -0.5 * LOG2E, scale_c=HD ** -0.5 * LOG2E),
        grid=(b, s // ts),
        in_specs=in_specs,
        out_specs=out_specs,
        out_shape=out_shape,
        compiler_params=_cparams(("parallel", "parallel")),
        name="proj",
    )(x3d, *consts, tab)


def _attn_kernel(qt_ref, k_ref, vt_ref, ot_ref, m_ref, acc_ref, *, tk):
    s = k_ref.shape[0]
    qt = qt_ref[...]
    m_ref[...] = jnp.full_like(m_ref, NEG_INF)
    acc_ref[...] = jnp.zeros_like(acc_ref)

    def body(j, carry):
        off = pl.multiple_of(j * tk, tk)
        st = jnp.dot(k_ref[pl.ds(off, tk), :], qt, preferred_element_type=F32)
        m_old = m_ref[...]
        m_new = jnp.maximum(m_old, jnp.max(st, axis=0, keepdims=True))
        p = jnp.exp2(st - m_new).astype(BF16)
        acc_ref[...] = jnp.exp2(m_old - m_new) * acc_ref[...] + jnp.dot(
            vt_ref[:, pl.ds(off, tk)], p, preferred_element_type=F32)
        m_ref[...] = m_new
        return carry

    lax.fori_loop(0, s // tk, body, 0)
    acc = acc_ref[...]
    ot_ref[...] = (acc[:V_A] / acc[V_A:V_A + 1]).astype(ot_ref.dtype)


def _attention(qt, k, vt, tq=512, tk=256):
    b, hq, _, s = qt.shape
    rep = hq // k.shape[1]
    tq = min(tq, s)
    tk = min(tk, s)
    return pl.pallas_call(
        functools.partial(_attn_kernel, tk=tk),
        grid=(b, hq, s // tq),
        in_specs=[pl.BlockSpec((None, None, QK_PAD, tq), lambda bi, h, i: (bi, h, 0, i)),
                  pl.BlockSpec((None, None, s, QK_PAD), lambda bi, h, i: (bi, h // rep, 0, 0)),
                  pl.BlockSpec((None, None, V_ROWS, s), lambda bi, h, i: (bi, h // rep, 0, 0))],
        out_specs=pl.BlockSpec((None, None, V_A, tq), lambda bi, h, i: (bi, h, 0, i)),
        out_shape=jax.ShapeDtypeStruct((b, hq, V_A, s), BF16),
        scratch_shapes=[pltpu.VMEM((1, tq), F32), pltpu.VMEM((V_ROWS, tq), F32)],
        compiler_params=_cparams(("parallel", "parallel", "arbitrary")),
        name="attention",
    )(qt, k, vt)


def _dilated_kernel(q_ref, kp_ref, kc_ref, kn_ref, vp_ref, vc_ref, vn_ref, bias_ref, o_ref, lse_ref, kbuf, vbuf,
                    *, n_l):
    lblk = q_ref.shape[0]
    r = BAND_R
    li = pl.program_id(2)
    kbuf[0:r] = kp_ref[...]
    kbuf[r:r + lblk] = kc_ref[...]
    kbuf[r + lblk:] = kn_ref[...]
    vbuf[0:r] = vp_ref[...]
    vbuf[r:r + lblk] = vc_ref[...]
    vbuf[r + lblk:] = vn_ref[...]
    col = lax.broadcasted_iota(jnp.int32, (1, 3 * r), 1)
    edge_lo = jnp.where((col < r) & (li == 0), NEG_INF, 0.0)
    edge_hi = jnp.where((col >= 2 * r) & (li == n_l - 1), NEG_INF, 0.0)
    head_of_lane = lax.broadcasted_iota(jnp.int32, (1, GROUP_B_COLS), 1) // HD
    n_blk = lblk // r
    for c in range(n_blk):
        q = q_ref[c * r:(c + 1) * r, :]
        kw = kbuf[c * r:(c + 3) * r, :]
        vw = vbuf[c * r:(c + 3) * r, :]
        o_acc = jnp.zeros((r, GROUP_B_COLS), F32)
        lse_acc = jnp.zeros((r, GROUP_B_COLS), F32)
        for h in range(HPG_B):
            sel = head_of_lane == h
            qh = jnp.where(sel, q, jnp.zeros_like(q))
            s = lax.dot_general(qh, kw, (((1,), (1,)), ((), ())), preferred_element_type=F32)
            s = s * (HD ** -0.5) + bias_ref[h]
            if c == 0:
                s = s + edge_lo
            if c == n_blk - 1:
                s = s + edge_hi
            m = jnp.max(s, axis=-1, keepdims=True)
            p = jnp.exp(s - m)
            l = jnp.sum(p, axis=-1, keepdims=True)
            pn = (p / l).astype(BF16)
            vh = jnp.where(sel, vw, jnp.zeros_like(vw))
            o_acc = o_acc + jnp.dot(pn, vh, preferred_element_type=F32)
            lse_acc = lse_acc + jnp.where(sel, m + jnp.log(l), 0.0)
        o_ref[c * r:(c + 1) * r, :] = o_acc.astype(o_ref.dtype)
        lse_ref[c * r:(c + 1) * r, :] = lse_acc


def _dilated_group(qkvb, bias, g, dil, lblk=256):
    b, s, _ = qkvb.shape
    l = s // dil
    lblk = min(lblk, l)
    n_l = l // lblk
    nb = lblk // BAND_R
    n_kb = l // BAND_R
    cols_per_tok = B_COLS // GROUP_B_COLS
    x = qkvb.reshape(b, l, dil * B_COLS)

    def cur(part):
        return pl.BlockSpec((None, lblk, GROUP_B_COLS),
                            lambda bi, r, li: (bi, li, r * cols_per_tok + part * N_GROUPS_B + g))

    def prev(part):
        return pl.BlockSpec((None, BAND_R, GROUP_B_COLS),
                            lambda bi, r, li: (bi, jnp.maximum(li * nb - 1, 0), r * cols_per_tok + part * N_GROUPS_B + g))

    def nxt(part):
        return pl.BlockSpec((None, BAND_R, GROUP_B_COLS),
                            lambda bi, r, li: (bi, jnp.minimum((li + 1) * nb, n_kb - 1),
                                               r * cols_per_tok + part * N_GROUPS_B + g))

    out_spec = pl.BlockSpec((None, lblk, GROUP_B_COLS), lambda bi, r, li: (bi, li, r))
    o, lse = pl.pallas_call(
        functools.partial(_dilated_kernel, n_l=n_l),
        grid=(b, dil, n_l),
        in_specs=[cur(0), prev(1), cur(1), nxt(1), prev(2), cur(2), nxt(2),
                  pl.BlockSpec(bias.shape, lambda bi, r, li: (0, 0, 0))],
        out_specs=[out_spec, out_spec],
        out_shape=[jax.ShapeDtypeStruct((b, l, dil * GROUP_B_COLS), BF16),
                   jax.ShapeDtypeStruct((b, l, dil * GROUP_B_COLS), F32)],
        scratch_shapes=[pltpu.VMEM((lblk + 2 * BAND_R, GROUP_B_COLS), BF16),
                        pltpu.VMEM((lblk + 2 * BAND_R, GROUP_B_COLS), BF16)],
        compiler_params=_cparams(("parallel", "parallel", "arbitrary")),
        name=f"dilated_d{dil}",
    )(x, x, x, x, x, x, x, bias)
    return o.reshape(b, s, GROUP_B_COLS), lse.reshape(b, s, GROUP_B_COLS)


def _merge_kernel(x_ref, gmix_ref, oat_ref, oct_ref, ob0_ref, ob1_ref, ob2_ref, l0_ref, l1_ref, l2_ref,
                  wg_ref, bg_ref, woa_ref, wob_ref, woc_ref, wout_ref, o_ref):
    x = x_ref[...]
    ub = _rms_rows(x, gmix_ref[...]).astype(BF16)
    d = x.shape[1]

    l0, l1, l2 = l0_ref[...], l1_ref[...], l2_ref[...]
    m = jnp.maximum(jnp.maximum(l0, l1), l2)
    e0, e1, e2 = jnp.exp(l0 - m), jnp.exp(l1 - m), jnp.exp(l2 - m)
    inv = 1.0 / (e0 + e1 + e2)
    ob = ((e0 * inv) * ob0_ref[...].astype(F32) + (e1 * inv) * ob1_ref[...].astype(F32)
          + (e2 * inv) * ob2_ref[...].astype(F32))

    tn = (((0,), (0,)), ((), ()))
    ya = lax.dot_general(oat_ref[...], woa_ref[...], tn, preferred_element_type=F32)
    yb = jnp.dot(ob.astype(BF16), wob_ref[...], preferred_element_type=F32)
    yc = lax.dot_general(oct_ref[...], woc_ref[...], tn, preferred_element_type=F32)

    def gate(n):
        z = jnp.dot(ub, wg_ref[:, n * d:(n + 1) * d], preferred_element_type=F32) + bg_ref[:, n * d:(n + 1) * d]
        return jax.nn.sigmoid(z)

    merged = gate(0) * ya + gate(1) * yb + gate(2) * yc
    o_ref[...] = x + jnp.dot(merged.astype(BF16), wout_ref[...], preferred_element_type=F32)


def _merge(x3d, oat, oct, obs, lses, lw, ts=512):
    b, s, d = x3d.shape
    ts = min(ts, s)
    tok = lambda w: pl.BlockSpec((None, ts, w), lambda bi, i: (bi, i, 0))
    feat = lambda r: pl.BlockSpec((None, r, ts), lambda bi, i: (bi, 0, i))
    consts = [lw["w_gate"], lw["b_gate"], lw["w_oa"], lw["w_ob"], lw["w_oc"], lw["w_out"]]
    return pl.pallas_call(
        _merge_kernel,
        grid=(b, s // ts),
        in_specs=[tok(d), _const_spec(lw["g_mix"].shape), feat(HA * V_A), feat(HC * HD)]
                 + [tok(GROUP_B_COLS)] * 6 + [_const_spec(c.shape) for c in consts],
        out_specs=tok(d),
        out_shape=jax.ShapeDtypeStruct((b, s, d), F32),
        compiler_params=_cparams(("parallel", "parallel")),
        name="merge",
    )(x3d, lw["g_mix"], oat.reshape(b, HA * V_A, s), oct.reshape(b, HC * HD, s), *obs, *lses, *consts)


def _t5_bucket(rel):
    nb = T5_BUCKETS // 2
    max_exact = nb // 2
    n = np.abs(rel)
    large = max_exact + (np.log(np.maximum(n, 1) / max_exact) / math.log(T5_MAX_DIST / max_exact)
                         * (nb - max_exact)).astype(np.int32)
    large = np.minimum(large, nb - 1)
    return (rel > 0).astype(np.int32) * nb + np.where(n < max_exact, n, large).astype(np.int32)


def _band_bias(rel_bias):
    qi = np.arange(BAND_R)[:, None]
    kj = np.arange(3 * BAND_R)[None, :]
    rel = kj - BAND_R - qi
    band = np.where(np.abs(rel) <= BAND_R, 0.0, NEG_INF).astype(np.float32)
    out = []
    for g, (_, dil) in enumerate(DIL_PAIRS):
        tab = rel_bias[:, g * HPG_B:(g + 1) * HPG_B]
        bias = jnp.transpose(tab[_t5_bucket(rel * dil)], (2, 0, 1)).astype(F32)
        out.append(bias + band[None])
    return out


def _rope_tables(s):
    freqs = (ROPE_THETA ** (-np.arange(ROPE_HALF) / ROPE_HALF)).astype(np.float32)
    pos = jnp.arange(s, dtype=jnp.int32)
    rows = jnp.repeat(jnp.arange(s // GRID_W, dtype=jnp.int32), GRID_W)
    cols = pos % GRID_W
    tabs = []
    for p in (pos, rows, cols):
        ang = p.astype(F32)[:, None] * jnp.asarray(freqs)[None, :]
        tabs += [jnp.cos(ang).T, jnp.sin(ang).T]
    return jnp.stack(tabs, axis=0)


def _ffn_weights(w_in, w_out):
    d = w_in.shape[0]
    n_chunks = D_FF // FFN_CHUNK
    w_in_c = w_in.astype(BF16).reshape(d, 2 * n_chunks, FFN_CHUNK).transpose(1, 0, 2)
    w_out_c = w_out.astype(BF16).reshape(n_chunks, FFN_CHUNK, d)
    return w_in_c, w_out_c


def _layer_weights(W, i):
    row = lambda v: v.reshape(1, -1).astype(F32)
    colv = lambda v: v.reshape(-1, 1).astype(F32)
    w_in = W["w_in"][i]
    w_a = w_in[:, :A_COLS]
    w_b = w_in[:, A_COLS:A_COLS + B_COLS]
    w_c = w_in[:, A_COLS + B_COLS:]
    head = np.arange(GROUP_B_COLS) // HD
    ones_blk = jnp.asarray((head[:, None] == head[None, :]).astype(np.float32) / HD, dtype=BF16)
    lw = dict(
        g_ffn1=row(W["g_ffn1"][i]), g_ffn2=row(W["g_ffn2"][i]), g_mix=row(W["g_mix"][i]), g_ple=row(W["g_ple"][i]),
        w_b=w_b.astype(BF16),
        w_act=jnp.concatenate([w_a, w_c], axis=1).T.astype(BF16),
        ones_blk=ones_blk,
        g_qb=row(jnp.tile(W["g_qb"][i], HPG_B)), g_kb=row(jnp.tile(W["g_kb"][i], HPG_B)),
        g_cq=colv(W["g_cq"][i]), g_ckv=colv(W["g_ckv"][i]),
        w_uqt=W["w_uq"][i].T.astype(BF16), w_ukvt=W["w_ukv"][i].T.astype(BF16),
        g_qa=colv(W["g_qa"][i]), g_ka=colv(W["g_ka"][i]), g_qc=colv(W["g_qc"][i]), g_kc=colv(W["g_kc"][i]),
        w_gate=W["w_gate"][i].astype(BF16), b_gate=row(W["b_gate"][i]),
        w_oa=W["w_oa"][i].astype(BF16), w_ob=W["w_ob"][i].astype(BF16), w_oc=W["w_oc"][i].astype(BF16),
        w_out=W["w_out"][i].astype(BF16),
        w_pg=W["w_pg"][i].astype(BF16), w_ple=W["w_ple"][i].astype(BF16),
    )
    lw["ffn1"] = _ffn_weights(W["w_ffn1_in"][i], W["w_ffn1_out"][i])
    lw["ffn2"] = _ffn_weights(W["w_ffn2_in"][i], W["w_ffn2_out"][i])
    return lw


def _trunk(x, p, layers, biases):
    b, s, d = x.shape
    tab = _rope_tables(s)
    for i, lw in enumerate(layers):
        x1 = _ffn(x.reshape(b * s, d), lw["g_ffn1"], *lw["ffn1"]).reshape(b, s, d)
        qkvb, qat, ka, vat, qct, kc, vct = _proj(x1, lw, tab)
        oat = _attention(qat, ka, vat)
        oct = _attention(qct, kc, vct)
        obs, lses = [], []
        for g, (_, dil) in enumerate(DIL_PAIRS):
            o, lse = _dilated_group(qkvb, biases[g], g, dil)
            obs.append(o)
            lses.append(lse)
        x2 = _merge(x1, oat, oct, obs, lses, lw)
        x = _ffn(x2.reshape(b * s, d), lw["g_ffn2"], *lw["ffn2"],
                 ple_args=(p[i].reshape(b * s, PLE_DIM), lw["g_ple"], lw["w_pg"], lw["w_ple"])).reshape(b, s, d)
    return x


def kernel(x_prompt, x_sample, p_prompt, p_sample, g_ffn1, w_ffn1_in, w_ffn1_out, g_mix, w_in, g_cq, g_ckv, w_uq,
           w_ukv, g_qa, g_ka, g_qb, g_kb, rel_bias, g_qc, g_kc, w_gate, b_gate, w_oa, w_ob, w_oc, w_out, g_ffn2,
           w_ffn2_in, w_ffn2_out, g_ple, w_pg, w_ple):
    W = dict(g_ffn1=g_ffn1, w_ffn1_in=w_ffn1_in, w_ffn1_out=w_ffn1_out, g_mix=g_mix, w_in=w_in,
             g_cq=g_cq, g_ckv=g_ckv, w_uq=w_uq, w_ukv=w_ukv, g_qa=g_qa, g_ka=g_ka,
             g_qb=g_qb, g_kb=g_kb, g_qc=g_qc, g_kc=g_kc,
             w_gate=w_gate, b_gate=b_gate, w_oa=w_oa, w_ob=w_ob, w_oc=w_oc, w_out=w_out,
             g_ffn2=g_ffn2, w_ffn2_in=w_ffn2_in, w_ffn2_out=w_ffn2_out,
             g_ple=g_ple, w_pg=w_pg, w_ple=w_ple)
    layers = [_layer_weights(W, i) for i in range(g_mix.shape[0])]
    biases = _band_bias(rel_bias)
    return (_trunk(x_prompt, p_prompt, layers, biases), _trunk(x_sample, p_sample, layers, biases))
```

```python
import functools
import math

import numpy as np
import jax
import jax.numpy as jnp
from jax import lax
from jax.experimental import pallas as pl
from jax.experimental.pallas import tpu as pltpu

F32 = jnp.float32
BF16 = jnp.bfloat16

D_MODEL = 1024
GRID_W = 64
PLE_DIM = 256
D_FF = 2816
NORM_EPS = 1e-6
ROPE_THETA = 10000.0
NEG_INF = -1e30

HA = 8
Q_RANK = 256
KV_RANK = 128
NOPE_A = 64
ROPE_A = 32
V_A = 64
QK_A = NOPE_A + ROPE_A

HD = 64
DIL_PAIRS = ((128, 1), (512, 4), (2048, 16))
N_GROUPS_B = 3
HPG_B = 4
N_HEADS_B = N_GROUPS_B * HPG_B
T5_BUCKETS = 32
T5_MAX_DIST = 1024
BAND_R = 64

HC = 8
KVC = 2
GC = HC // KVC

A_COLS = Q_RANK + KV_RANK + ROPE_A
B_COLS = 3 * N_HEADS_B * HD
C_COLS = (HC + 2 * KVC) * HD
GROUP_B_COLS = HPG_B * HD

ROPE_HALF = 16
QK_PAD = 128
V_ROWS = 80
LOG2E = math.log2(math.e)

VMEM_LIMIT_BYTES = 56 * 1024 * 1024
FFN_CHUNK = 256
SCORE_LOOKAHEAD = 2
DENOM_FLOOR = 2.0 ** -80


def _cparams(semantics):
    return pltpu.CompilerParams(dimension_semantics=semantics, vmem_limit_bytes=VMEM_LIMIT_BYTES)


def _const_spec(shape):
    nd = len(shape)
    return pl.BlockSpec(shape, lambda *_: (0,) * nd, pipeline_mode=pl.Buffered(1))


def _rms_rows(x, g):
    ms = jnp.mean(x * x, axis=-1, keepdims=True)
    return x * lax.rsqrt(ms + NORM_EPS) * g


def _rms_cols(x, g):
    ms = jnp.mean(x * x, axis=0, keepdims=True)
    return x * lax.rsqrt(ms + NORM_EPS) * g


def _rope_cols(x, cos, sin):
    x1, x2 = x[:ROPE_HALF], x[ROPE_HALF:]
    return x1 * cos - x2 * sin, x1 * sin + x2 * cos


def _ffn_kernel(*refs, n_chunks, ple):
    if ple:
        x_ref, g_ref, win_ref, wout_ref, pe_ref, gple_ref, wpg_ref, wple_ref, o_ref, acc_ref = refs
    else:
        x_ref, g_ref, win_ref, wout_ref, o_ref, acc_ref = refs
    x = x_ref[...]
    u = _rms_rows(x, g_ref[...]).astype(BF16)
    acc_ref[...] = jnp.zeros_like(acc_ref)

    def body(c, carry):
        a = jnp.dot(u, win_ref[c], preferred_element_type=F32)
        b = jnp.dot(u, win_ref[n_chunks + c], preferred_element_type=F32)
        h = (a * jax.nn.sigmoid(a) * b).astype(BF16)
        acc_ref[...] += jnp.dot(h, wout_ref[c], preferred_element_type=F32)
        return carry

    lax.fori_loop(0, n_chunks, body, 0)
    y = x + 0.5 * acc_ref[...]
    if ple:
        gate = jax.nn.sigmoid(
            jnp.dot(_rms_rows(y, gple_ref[...]).astype(BF16), wpg_ref[...], preferred_element_type=F32))
        y = y + gate * jnp.dot(pe_ref[...].astype(BF16), wple_ref[...], preferred_element_type=F32)
    o_ref[...] = y


def _ffn(x2d, g, w_in_c, w_out_c, ple_args=None, tm=512):
    n, d = x2d.shape
    n_chunks = w_out_c.shape[0]
    tm = min(tm, n)
    row = lambda i: (i, 0)
    in_specs = [pl.BlockSpec((tm, d), row), _const_spec(g.shape), _const_spec(w_in_c.shape), _const_spec(w_out_c.shape)]
    args = [x2d, g, w_in_c, w_out_c]
    if ple_args is not None:
        pe2d, g_ple, w_pg, w_ple = ple_args
        in_specs += [pl.BlockSpec((tm, pe2d.shape[1]), row), _const_spec(g_ple.shape), _const_spec(w_pg.shape),
                     _const_spec(w_ple.shape)]
        args += [pe2d, g_ple, w_pg, w_ple]
    return pl.pallas_call(
        functools.partial(_ffn_kernel, n_chunks=n_chunks, ple=ple_args is not None),
        grid=(n // tm,),
        in_specs=in_specs,
        out_specs=pl.BlockSpec((tm, d), row),
        out_shape=jax.ShapeDtypeStruct((n, d), F32),
        scratch_shapes=[pltpu.VMEM((tm, d), F32)],
        compiler_params=_cparams(("parallel",)),
        name="ffn_ple" if ple_args is not None else "ffn",
    )(*args)


_T_CQ = 0
_T_CKV = Q_RANK
_T_KR = Q_RANK + KV_RANK
_T_QC = A_COLS
_T_KC = A_COLS + HC * HD
_T_VC = A_COLS + (HC + KVC) * HD
_T_ROWS = A_COLS + C_COLS


def _proj_kernel(x_ref, gmix_ref, wb_ref, wact_ref, ones_ref, gqb_ref, gkb_ref, gcq_ref, gckv_ref, wuqt_ref,
                 wukvt_ref, gqa_ref, gka_ref, gqc_ref, gkc_ref, tab_ref,
                 qkvb_ref, qat_ref, ka_ref, vat_ref, kna_ref, qct_ref, kc_ref, vct_ref, knc_ref, *, scale_a, scale_c):
    ts = x_ref.shape[0]

    def key_tail(rows):
        r = lax.broadcasted_iota(jnp.int32, (rows, ts), 0)
        return jnp.where(r == rows - 1, 1.0, 0.0).astype(F32)

    def key_norm(kt):
        return jnp.sqrt(jnp.sum(kt * kt, axis=0, keepdims=True))

    ub = _rms_rows(x_ref[...], gmix_ref[...]).astype(BF16)

    pb = jnp.dot(ub, wb_ref[...], preferred_element_type=F32)
    ones = ones_ref[...]
    n_qk = N_HEADS_B * HD
    for part, g_ref in ((0, gqb_ref), (1, gkb_ref)):
        for c in range(N_GROUPS_B):
            lo = part * n_qk + c * GROUP_B_COLS
            xx = pb[:, lo:lo + GROUP_B_COLS]
            ms = jnp.dot((xx * xx).astype(BF16), ones, preferred_element_type=F32)
            qkvb_ref[:, lo:lo + GROUP_B_COLS] = (xx * lax.rsqrt(ms + NORM_EPS) * g_ref[...]).astype(BF16)
    qkvb_ref[:, 2 * n_qk:] = pb[:, 2 * n_qk:].astype(BF16)

    pt = lax.dot_general(wact_ref[...], ub, (((1,), (1,)), ((), ())), preferred_element_type=F32)
    cos_p, sin_p, cos_r, sin_r, cos_c, sin_c = (tab_ref[i] for i in range(6))
    ones_rows = jnp.ones((V_ROWS - V_A, ts), BF16)

    cq = _rms_cols(pt[_T_CQ:_T_CQ + Q_RANK], gcq_ref[...]).astype(BF16)
    qa = jnp.dot(wuqt_ref[...], cq, preferred_element_type=F32)
    ckv = _rms_cols(pt[_T_CKV:_T_CKV + KV_RANK], gckv_ref[...]).astype(BF16)
    kv = jnp.dot(wukvt_ref[...], ckv, preferred_element_type=F32)
    gqa = gqa_ref[...]
    gka = gka_ref[...]
    kr1, kr2 = _rope_cols(_rms_cols(pt[_T_KR:_T_KR + ROPE_A], gka[NOPE_A:]), cos_p, sin_p)
    k_tail = key_tail(QK_PAD - QK_A)
    kr_sq = jnp.sum(kr1 * kr1 + kr2 * kr2, axis=0, keepdims=True)
    for h in range(HA):
        q0 = h * QK_A
        qn = _rms_cols(qa[q0:q0 + NOPE_A], gqa[:NOPE_A])
        q1, q2 = _rope_cols(_rms_cols(qa[q0 + NOPE_A:q0 + QK_A], gqa[NOPE_A:]), cos_p, sin_p)
        qat_ref[h, 0:NOPE_A] = (qn * scale_a).astype(BF16)
        qat_ref[h, NOPE_A:NOPE_A + ROPE_HALF] = (q1 * scale_a).astype(BF16)
        qat_ref[h, NOPE_A + ROPE_HALF:QK_A] = (q2 * scale_a).astype(BF16)
        qat_ref[h, QK_A:QK_PAD] = jnp.zeros((QK_PAD - QK_A, ts), BF16)
        k0 = h * (NOPE_A + V_A)
        kn = _rms_cols(kv[k0:k0 + NOPE_A], gka[:NOPE_A])
        kt = jnp.concatenate([kn, kr1, kr2, k_tail], axis=0)
        ka_ref[h] = kt.T.astype(BF16)
        kna_ref[h] = jnp.sqrt(jnp.sum(kn * kn, axis=0, keepdims=True) + kr_sq)
        vat_ref[h, 0:V_A] = kv[k0 + NOPE_A:k0 + NOPE_A + V_A].astype(BF16)
        vat_ref[h, V_A:V_ROWS] = ones_rows

    def axial(t):
        a1, a2 = _rope_cols(t[:HD // 2], cos_r, sin_r)
        b1, b2 = _rope_cols(t[HD // 2:], cos_c, sin_c)
        return a1, a2, b1, b2

    gqc = gqc_ref[...]
    gkc = gkc_ref[...]
    for h in range(HC):
        parts = axial(_rms_cols(pt[_T_QC + h * HD:_T_QC + (h + 1) * HD], gqc) * scale_c)
        qct_ref[h, 0:HD] = jnp.concatenate(parts, axis=0).astype(BF16)
        qct_ref[h, HD:QK_PAD] = jnp.zeros((QK_PAD - HD, ts), BF16)
    c_tail = key_tail(QK_PAD - HD)
    for h in range(KVC):
        kh = jnp.concatenate(axial(_rms_cols(pt[_T_KC + h * HD:_T_KC + (h + 1) * HD], gkc)), axis=0)
        kt = jnp.concatenate([kh, c_tail], axis=0)
        kc_ref[h] = kt.T.astype(BF16)
        knc_ref[h] = key_norm(kh)
        vct_ref[h, 0:HD] = pt[_T_VC + h * HD:_T_VC + (h + 1) * HD].astype(BF16)
        vct_ref[h, HD:V_ROWS] = ones_rows


def _proj(x3d, lw, tab, ts=512):
    b, s, d = x3d.shape
    ts = min(ts, s)
    consts = [lw["g_mix"], lw["w_b"], lw["w_act"], lw["ones_blk"], lw["g_qb"], lw["g_kb"], lw["g_cq"], lw["g_ckv"],
              lw["w_uqt"], lw["w_ukvt"], lw["g_qa"], lw["g_ka"], lw["g_qc"], lw["g_kc"]]
    in_specs = ([pl.BlockSpec((None, ts, d), lambda bi, i: (bi, i, 0))] + [_const_spec(c.shape) for c in consts]
                + [pl.BlockSpec((6, ROPE_HALF, ts), lambda bi, i: (0, 0, i))])
    tok_major = lambda h: pl.BlockSpec((None, h, ts, QK_PAD), lambda bi, i: (bi, 0, i, 0))
    feat_major = lambda h, r: pl.BlockSpec((None, h, r, ts), lambda bi, i: (bi, 0, 0, i))
    out_specs = [pl.BlockSpec((None, ts, B_COLS), lambda bi, i: (bi, i, 0)),
                 feat_major(HA, QK_PAD), tok_major(HA), feat_major(HA, V_ROWS), feat_major(HA, 1),
                 feat_major(HC, QK_PAD), tok_major(KVC), feat_major(KVC, V_ROWS), feat_major(KVC, 1)]
    out_shape = [jax.ShapeDtypeStruct((b, s, B_COLS), BF16),
                 jax.ShapeDtypeStruct((b, HA, QK_PAD, s), BF16), jax.ShapeDtypeStruct((b, HA, s, QK_PAD), BF16),
                 jax.ShapeDtypeStruct((b, HA, V_ROWS, s), BF16), jax.ShapeDtypeStruct((b, HA, 1, s), F32),
                 jax.ShapeDtypeStruct((b, HC, QK_PAD, s), BF16), jax.ShapeDtypeStruct((b, KVC, s, QK_PAD), BF16),
                 jax.ShapeDtypeStruct((b, KVC, V_ROWS, s), BF16), jax.ShapeDtypeStruct((b, KVC, 1, s), F32)]
    return pl.pallas_call(
        functools.partial(_proj_kernel, scale_a=QK_A ** -0.5 * LOG2E, scale_c=HD ** -0.5 * LOG2E),
        grid=(b, s // ts),
        in_specs=in_specs,
        out_specs=out_specs,
        out_shape=out_shape,
        compiler_params=_cparams(("parallel", "parallel")),
        name="proj",
    )(x3d, *consts, tab)


def _attn_kernel(kmax_ref, qt_ref, k_ref, vt_ref, ot_ref, acc_ref, m_ref, *, tk, n_sub):
    s_len = k_ref.shape[0]
    qt = qt_ref[...]
    qf = qt.astype(F32)
    qn = jnp.sqrt(jnp.sum(qf * qf, axis=0, keepdims=True))
    bound = qn * kmax_ref[0:1, 0:1] * (1.0 + 2.0 ** -6) + 1.0
    row = lax.broadcasted_iota(jnp.int32, qt.shape, 0)
    qa = jnp.where(row == QK_PAD - 1, -bound, qf).astype(BF16)

    acc_ref[...] = jnp.zeros_like(acc_ref)
    span = tk * n_sub

    def bounded(jj, carry):
        off = jj * span
        chunk = lambda u: pl.ds(pl.multiple_of(off + u * tk, tk), tk)
        score = lambda u: jnp.dot(k_ref[chunk(u), :], qa, preferred_element_type=F32)
        st = [score(u) for u in range(min(SCORE_LOOKAHEAD, n_sub))]
        tot = None
        for u in range(n_sub):
            if u + SCORE_LOOKAHEAD < n_sub:
                st.append(score(u + SCORE_LOOKAHEAD))
            pv = jnp.dot(vt_ref[:, chunk(u)], jnp.exp2(st[u]).astype(BF16), preferred_element_type=F32)
            tot = pv if tot is None else tot + pv
        acc_ref[...] += tot
        return carry

    lax.fori_loop(0, s_len // span, bounded, 0)
    acc = acc_ref[...]
    denom = acc[V_A:V_A + 1]
    trusted = jnp.min(denom) >= DENOM_FLOOR

    @pl.when(trusted)
    def _():
        ot_ref[...] = (acc[:V_A] / denom).astype(ot_ref.dtype)

    @pl.when(jnp.logical_not(trusted))
    def _():
        m_ref[...] = jnp.full_like(m_ref, NEG_INF)
        acc_ref[...] = jnp.zeros_like(acc_ref)

        def running_max(j, carry):
            o = pl.multiple_of(j * tk, tk)
            st = jnp.dot(k_ref[pl.ds(o, tk), :], qt, preferred_element_type=F32)
            m_old = m_ref[...]
            m_new = jnp.maximum(m_old, jnp.max(st, axis=0, keepdims=True))
            p = jnp.exp2(st - m_new).astype(BF16)
            acc_ref[...] = jnp.exp2(m_old - m_new) * acc_ref[...] + jnp.dot(
                vt_ref[:, pl.ds(o, tk)], p, preferred_element_type=F32)
            m_ref[...] = m_new
            return carry

        lax.fori_loop(0, s_len // tk, running_max, 0)
        a = acc_ref[...]
        ot_ref[...] = (a[:V_A] / a[V_A:V_A + 1]).astype(ot_ref.dtype)


def _attention(qt, k, vt, knorm, tq=512, tk=256, n_sub=16):
    b, hq, _, s = qt.shape
    hk = k.shape[1]
    rep = hq // hk
    tq = min(tq, s)
    tk = min(tk, s)
    n_sub = min(n_sub, s // tk)
    kmax = jnp.broadcast_to(jnp.max(knorm, axis=(2, 3))[:, :, None, None], (b, hk, 8, 128))
    return pl.pallas_call(
        functools.partial(_attn_kernel, tk=tk, n_sub=n_sub),
        grid=(b, hq, s // tq),
        in_specs=[pl.BlockSpec((None, None, 8, 128), lambda bi, h, i: (bi, h // rep, 0, 0)),
                  pl.BlockSpec((None, None, QK_PAD, tq), lambda bi, h, i: (bi, h, 0, i)),
                  pl.BlockSpec((None, None, s, QK_PAD), lambda bi, h, i: (bi, h // rep, 0, 0)),
                  pl.BlockSpec((None, None, V_ROWS, s), lambda bi, h, i: (bi, h // rep, 0, 0))],
        out_specs=pl.BlockSpec((None, None, V_A, tq), lambda bi, h, i: (bi, h, 0, i)),
        out_shape=jax.ShapeDtypeStruct((b, hq, V_A, s), BF16),
        scratch_shapes=[pltpu.VMEM((V_ROWS, tq), F32), pltpu.VMEM((1, tq), F32)],
        compiler_params=_cparams(("parallel", "parallel", "arbitrary")),
        name="attention",
    )(kmax, qt, k, vt)


def _dilated_kernel(q_ref, kp_ref, kc_ref, kn_ref, vp_ref, vc_ref, vn_ref, bias_ref, o_ref, lse_ref, kbuf, vbuf,
                    *, n_l):
    lblk = q_ref.shape[0]
    r = BAND_R
    li = pl.program_id(2)
    kbuf[0:r] = kp_ref[...]
    kbuf[r:r + lblk] = kc_ref[...]
    kbuf[r + lblk:] = kn_ref[...]
    vbuf[0:r] = vp_ref[...]
    vbuf[r:r + lblk] = vc_ref[...]
    vbuf[r + lblk:] = vn_ref[...]
    col = lax.broadcasted_iota(jnp.int32, (1, 3 * r), 1)
    edge_lo = jnp.where((col < r) & (li == 0), NEG_INF, 0.0)
    edge_hi = jnp.where((col >= 2 * r) & (li == n_l - 1), NEG_INF, 0.0)
    head_of_lane = lax.broadcasted_iota(jnp.int32, (1, GROUP_B_COLS), 1) // HD
    n_blk = lblk // r
    for c in range(n_blk):
        q = q_ref[c * r:(c + 1) * r, :]
        kw = kbuf[c * r:(c + 3) * r, :]
        vw = vbuf[c * r:(c + 3) * r, :]
        o_acc = jnp.zeros((r, GROUP_B_COLS), F32)
        lse_acc = jnp.zeros((r, GROUP_B_COLS), F32)
        for h in range(HPG_B):
            sel = head_of_lane == h
            qh = jnp.where(sel, q, jnp.zeros_like(q))
            s = lax.dot_general(qh, kw, (((1,), (1,)), ((), ())), preferred_element_type=F32)
            s = s * (HD ** -0.5) + bias_ref[h]
            if c == 0:
                s = s + edge_lo
            if c == n_blk - 1:
                s = s + edge_hi
            m = jnp.max(s, axis=-1, keepdims=True)
            p = jnp.exp(s - m)
            l = jnp.sum(p, axis=-1, keepdims=True)
            pn = (p / l).astype(BF16)
            vh = jnp.where(sel, vw, jnp.zeros_like(vw))
            o_acc = o_acc + jnp.dot(pn, vh, preferred_element_type=F32)
            lse_acc = lse_acc + jnp.where(sel, m + jnp.log(l), 0.0)
        o_ref[c * r:(c + 1) * r, :] = o_acc.astype(o_ref.dtype)
        lse_ref[c * r:(c + 1) * r, :] = lse_acc


def _dilated_group(qkvb, bias, g, dil, lblk=256):
    b, s, _ = qkvb.shape
    l = s // dil
    lblk = min(lblk, l)
    n_l = l // lblk
    nb = lblk // BAND_R
    n_kb = l // BAND_R
    cols_per_tok = B_COLS // GROUP_B_COLS
    x = qkvb.reshape(b, l, dil * B_COLS)

    def cur(part):
        return pl.BlockSpec((None, lblk, GROUP_B_COLS),
                            lambda bi, r, li: (bi, li, r * cols_per_tok + part * N_GROUPS_B + g))

    def prev(part):
        return pl.BlockSpec((None, BAND_R, GROUP_B_COLS),
                            lambda bi, r, li: (bi, jnp.maximum(li * nb - 1, 0), r * cols_per_tok + part * N_GROUPS_B + g))

    def nxt(part):
        return pl.BlockSpec((None, BAND_R, GROUP_B_COLS),
                            lambda bi, r, li: (bi, jnp.minimum((li + 1) * nb, n_kb - 1),
                                               r * cols_per_tok + part * N_GROUPS_B + g))

    out_spec = pl.BlockSpec((None, lblk, GROUP_B_COLS), lambda bi, r, li: (bi, li, r))
    o, lse = pl.pallas_call(
        functools.partial(_dilated_kernel, n_l=n_l),
        grid=(b, dil, n_l),
        in_specs=[cur(0), prev(1), cur(1), nxt(1), prev(2), cur(2), nxt(2),
                  pl.BlockSpec(bias.shape, lambda bi, r, li: (0, 0, 0))],
        out_specs=[out_spec, out_spec],
        out_shape=[jax.ShapeDtypeStruct((b, l, dil * GROUP_B_COLS), BF16),
                   jax.ShapeDtypeStruct((b, l, dil * GROUP_B_COLS), F32)],
        scratch_shapes=[pltpu.VMEM((lblk + 2 * BAND_R, GROUP_B_COLS), BF16),
                        pltpu.VMEM((lblk + 2 * BAND_R, GROUP_B_COLS), BF16)],
        compiler_params=_cparams(("parallel", "parallel", "arbitrary")),
        name=f"dilated_d{dil}",
    )(x, x, x, x, x, x, x, bias)
    return o.reshape(b, s, GROUP_B_COLS), lse.reshape(b, s, GROUP_B_COLS)


def _merge_kernel(x_ref, gmix_ref, oat_ref, oct_ref, ob0_ref, ob1_ref, ob2_ref, l0_ref, l1_ref, l2_ref,
                  wg_ref, bg_ref, woa_ref, wob_ref, woc_ref, wout_ref, o_ref):
    x = x_ref[...]
    ub = _rms_rows(x, gmix_ref[...]).astype(BF16)
    d = x.shape[1]

    l0, l1, l2 = l0_ref[...], l1_ref[...], l2_ref[...]
    m = jnp.maximum(jnp.maximum(l0, l1), l2)
    e0, e1, e2 = jnp.exp(l0 - m), jnp.exp(l1 - m), jnp.exp(l2 - m)
    inv = 1.0 / (e0 + e1 + e2)
    ob = ((e0 * inv) * ob0_ref[...].astype(F32) + (e1 * inv) * ob1_ref[...].astype(F32)
          + (e2 * inv) * ob2_ref[...].astype(F32))

    tn = (((0,), (0,)), ((), ()))
    ya = lax.dot_general(oat_ref[...], woa_ref[...], tn, preferred_element_type=F32)
    yb = jnp.dot(ob.astype(BF16), wob_ref[...], preferred_element_type=F32)
    yc = lax.dot_general(oct_ref[...], woc_ref[...], tn, preferred_element_type=F32)

    def gate(n):
        z = jnp.dot(ub, wg_ref[:, n * d:(n + 1) * d], preferred_element_type=F32) + bg_ref[:, n * d:(n + 1) * d]
        return jax.nn.sigmoid(z)

    merged = gate(0) * ya + gate(1) * yb + gate(2) * yc
    o_ref[...] = x + jnp.dot(merged.astype(BF16), wout_ref[...], preferred_element_type=F32)


def _merge(x3d, oat, oct, obs, lses, lw, ts=512):
    b, s, d = x3d.shape
    ts = min(ts, s)
    tok = lambda w: pl.BlockSpec((None, ts, w), lambda bi, i: (bi, i, 0))
    feat = lambda r: pl.BlockSpec((None, r, ts), lambda bi, i: (bi, 0, i))
    consts = [lw["w_gate"], lw["b_gate"], lw["w_oa"], lw["w_ob"], lw["w_oc"], lw["w_out"]]
    return pl.pallas_call(
        _merge_kernel,
        grid=(b, s // ts),
        in_specs=[tok(d), _const_spec(lw["g_mix"].shape), feat(HA * V_A), feat(HC * HD)]
                 + [tok(GROUP_B_COLS)] * 6 + [_const_spec(c.shape) for c in consts],
        out_specs=tok(d),
        out_shape=jax.ShapeDtypeStruct((b, s, d), F32),
        compiler_params=_cparams(("parallel", "parallel")),
        name="merge",
    )(x3d, lw["g_mix"], oat.reshape(b, HA * V_A, s), oct.reshape(b, HC * HD, s), *obs, *lses, *consts)


def _t5_bucket(rel):
    nb = T5_BUCKETS // 2
    max_exact = nb // 2
    n = np.abs(rel)
    large = max_exact + (np.log(np.maximum(n, 1) / max_exact) / math.log(T5_MAX_DIST / max_exact)
                         * (nb - max_exact)).astype(np.int32)
    large = np.minimum(large, nb - 1)
    return (rel > 0).astype(np.int32) * nb + np.where(n < max_exact, n, large).astype(np.int32)


def _band_bias(rel_bias):
    qi = np.arange(BAND_R)[:, None]
    kj = np.arange(3 * BAND_R)[None, :]
    rel = kj - BAND_R - qi
    band = np.where(np.abs(rel) <= BAND_R, 0.0, NEG_INF).astype(np.float32)
    out = []
    for g, (_, dil) in enumerate(DIL_PAIRS):
        tab = rel_bias[:, g * HPG_B:(g + 1) * HPG_B]
        bias = jnp.transpose(tab[_t5_bucket(rel * dil)], (2, 0, 1)).astype(F32)
        out.append(bias + band[None])
    return out


def _rope_tables(s):
    freqs = (ROPE_THETA ** (-np.arange(ROPE_HALF) / ROPE_HALF)).astype(np.float32)
    pos = jnp.arange(s, dtype=jnp.int32)
    rows = jnp.repeat(jnp.arange(s // GRID_W, dtype=jnp.int32), GRID_W)
    cols = pos % GRID_W
    tabs = []
    for p in (pos, rows, cols):
        ang = p.astype(F32)[:, None] * jnp.asarray(freqs)[None, :]
        tabs += [jnp.cos(ang).T, jnp.sin(ang).T]
    return jnp.stack(tabs, axis=0)


def _ffn_weights(w_in, w_out):
    d = w_in.shape[0]
    n_chunks = D_FF // FFN_CHUNK
    w_in_c = w_in.astype(BF16).reshape(d, 2 * n_chunks, FFN_CHUNK).transpose(1, 0, 2)
    w_out_c = w_out.astype(BF16).reshape(n_chunks, FFN_CHUNK, d)
    return w_in_c, w_out_c


def _layer_weights(W, i):
    row = lambda v: v.reshape(1, -1).astype(F32)
    colv = lambda v: v.reshape(-1, 1).astype(F32)
    w_in = W["w_in"][i]
    w_a = w_in[:, :A_COLS]
    w_b = w_in[:, A_COLS:A_COLS + B_COLS]
    w_c = w_in[:, A_COLS + B_COLS:]
    head = np.arange(GROUP_B_COLS) // HD
    ones_blk = jnp.asarray((head[:, None] == head[None, :]).astype(np.float32) / HD, dtype=BF16)
    lw = dict(
        g_ffn1=row(W["g_ffn1"][i]), g_ffn2=row(W["g_ffn2"][i]), g_mix=row(W["g_mix"][i]), g_ple=row(W["g_ple"][i]),
        w_b=w_b.astype(BF16),
        w_act=jnp.concatenate([w_a, w_c], axis=1).T.astype(BF16),
        ones_blk=ones_blk,
        g_qb=row(jnp.tile(W["g_qb"][i], HPG_B)), g_kb=row(jnp.tile(W["g_kb"][i], HPG_B)),
        g_cq=colv(W["g_cq"][i]), g_ckv=colv(W["g_ckv"][i]),
        w_uqt=W["w_uq"][i].T.astype(BF16), w_ukvt=W["w_ukv"][i].T.astype(BF16),
        g_qa=colv(W["g_qa"][i]), g_ka=colv(W["g_ka"][i]), g_qc=colv(W["g_qc"][i]), g_kc=colv(W["g_kc"][i]),
        w_gate=W["w_gate"][i].astype(BF16), b_gate=row(W["b_gate"][i]),
        w_oa=W["w_oa"][i].astype(BF16), w_ob=W["w_ob"][i].astype(BF16), w_oc=W["w_oc"][i].astype(BF16),
        w_out=W["w_out"][i].astype(BF16),
        w_pg=W["w_pg"][i].astype(BF16), w_ple=W["w_ple"][i].astype(BF16),
    )
    lw["ffn1"] = _ffn_weights(W["w_ffn1_in"][i], W["w_ffn1_out"][i])
    lw["ffn2"] = _ffn_weights(W["w_ffn2_in"][i], W["w_ffn2_out"][i])
    return lw


def _trunk(x, p, layers, biases):
    b, s, d = x.shape
    tab = _rope_tables(s)
    for i, lw in enumerate(layers):
        x1 = _ffn(x.reshape(b * s, d), lw["g_ffn1"], *lw["ffn1"]).reshape(b, s, d)
        qkvb, qat, ka, vat, kna, qct, kc, vct, knc = _proj(x1, lw, tab)
        oat = _attention(qat, ka, vat, kna)
        oct = _attention(qct, kc, vct, knc)
        obs, lses = [], []
        for g, (_, dil) in enumerate(DIL_PAIRS):
            o, lse = _dilated_group(qkvb, biases[g], g, dil)
            obs.append(o)
            lses.append(lse)
        x2 = _merge(x1, oat, oct, obs, lses, lw)
        x = _ffn(x2.reshape(b * s, d), lw["g_ffn2"], *lw["ffn2"],
                 ple_args=(p[i].reshape(b * s, PLE_DIM), lw["g_ple"], lw["w_pg"], lw["w_ple"])).reshape(b, s, d)
    return x


def kernel(x_prompt, x_sample, p_prompt, p_sample, g_ffn1, w_ffn1_in, w_ffn1_out, g_mix, w_in, g_cq, g_ckv, w_uq,
           w_ukv, g_qa, g_ka, g_qb, g_kb, rel_bias, g_qc, g_kc, w_gate, b_gate, w_oa, w_ob, w_oc, w_out, g_ffn2,
           w_ffn2_in, w_ffn2_out, g_ple, w_pg, w_ple):
    W = dict(g_ffn1=g_ffn1, w_ffn1_in=w_ffn1_in, w_ffn1_out=w_ffn1_out, g_mix=g_mix, w_in=w_in,
             g_cq=g_cq, g_ckv=g_ckv, w_uq=w_uq, w_ukv=w_ukv, g_qa=g_qa, g_ka=g_ka,
             g_qb=g_qb, g_kb=g_kb, g_qc=g_qc, g_kc=g_kc,
             w_gate=w_gate, b_gate=b_gate, w_oa=w_oa, w_ob=w_ob, w_oc=w_oc, w_out=w_out,
             g_ffn2=g_ffn2, w_ffn2_in=w_ffn2_in, w_ffn2_out=w_ffn2_out,
             g_ple=g_ple, w_pg=w_pg, w_ple=w_ple)
    layers = [_layer_weights(W, i) for i in range(g_mix.shape[0])]
    biases = _band_bias(rel_bias)
    return (_trunk(x_prompt, p_prompt, layers, biases), _trunk(x_sample, p_sample, layers, biases))
```

```python
import functools
import math

import numpy as np
import jax
import jax.numpy as jnp
from jax import lax
from jax.experimental import pallas as pl
from jax.experimental.pallas import tpu as pltpu

F32 = jnp.float32
BF16 = jnp.bfloat16

D_MODEL = 1024
GRID_W = 64
PLE_DIM = 256
D_FF = 2816
NORM_EPS = 1e-6
ROPE_THETA = 10000.0
NEG_INF = -1e30

HA = 8
Q_RANK = 256
KV_RANK = 128
NOPE_A = 64
ROPE_A = 32
V_A = 64
QK_A = NOPE_A + ROPE_A

HD = 64
DIL_PAIRS = ((128, 1), (512, 4), (2048, 16))
N_GROUPS_B = 3
HPG_B = 4
N_HEADS_B = N_GROUPS_B * HPG_B
T5_BUCKETS = 32
T5_MAX_DIST = 1024
BAND_R = 64
DIL_TILE = BAND_R * max(d for _, d in DIL_PAIRS)

HC = 8
KVC = 2
GC = HC // KVC

A_COLS = Q_RANK + KV_RANK + ROPE_A
B_COLS = 3 * N_HEADS_B * HD
C_COLS = (HC + 2 * KVC) * HD
GROUP_B_COLS = HPG_B * HD
LANES = 128
LANE_HALVES = GROUP_B_COLS // LANES

ROPE_HALF = 16
QK_PAD = 128
V_ROWS = 80
LOG2E = math.log2(math.e)

VMEM_LIMIT_BYTES = 56 * 1024 * 1024
FFN_CHUNK = 256
SCORE_LOOKAHEAD = 2
DENOM_FLOOR = 2.0 ** -80


def _cparams(semantics):
    return pltpu.CompilerParams(dimension_semantics=semantics, vmem_limit_bytes=VMEM_LIMIT_BYTES)


def _const_spec(shape):
    nd = len(shape)
    return pl.BlockSpec(shape, lambda *_: (0,) * nd, pipeline_mode=pl.Buffered(1))


def _rms_rows(x, g):
    ms = jnp.mean(x * x, axis=-1, keepdims=True)
    return x * lax.rsqrt(ms + NORM_EPS) * g


def _rms_cols(x, g):
    ms = jnp.mean(x * x, axis=0, keepdims=True)
    return x * lax.rsqrt(ms + NORM_EPS) * g


def _rope_cols(x, cos, sin):
    x1, x2 = x[:ROPE_HALF], x[ROPE_HALF:]
    return x1 * cos - x2 * sin, x1 * sin + x2 * cos


def _ffn_kernel(*refs, n_chunks, ple):
    if ple:
        x_ref, g_ref, win_ref, wout_ref, pe_ref, gple_ref, wpg_ref, wple_ref, o_ref, acc_ref = refs
    else:
        x_ref, g_ref, win_ref, wout_ref, o_ref, acc_ref = refs
    x = x_ref[...]
    u = _rms_rows(x, g_ref[...]).astype(BF16)
    acc_ref[...] = jnp.zeros_like(acc_ref)

    def body(c, carry):
        a = jnp.dot(u, win_ref[c], preferred_element_type=F32)
        b = jnp.dot(u, win_ref[n_chunks + c], preferred_element_type=F32)
        h = (a * jax.nn.sigmoid(a) * b).astype(BF16)
        acc_ref[...] += jnp.dot(h, wout_ref[c], preferred_element_type=F32)
        return carry

    lax.fori_loop(0, n_chunks, body, 0)
    y = x + 0.5 * acc_ref[...]
    if ple:
        gate = jax.nn.sigmoid(
            jnp.dot(_rms_rows(y, gple_ref[...]).astype(BF16), wpg_ref[...], preferred_element_type=F32))
        y = y + gate * jnp.dot(pe_ref[...].astype(BF16), wple_ref[...], preferred_element_type=F32)
    o_ref[...] = y


def _ffn(x2d, g, w_in_c, w_out_c, ple_args=None, tm=512):
    n, d = x2d.shape
    n_chunks = w_out_c.shape[0]
    tm = min(tm, n)
    row = lambda i: (i, 0)
    in_specs = [pl.BlockSpec((tm, d), row), _const_spec(g.shape), _const_spec(w_in_c.shape), _const_spec(w_out_c.shape)]
    args = [x2d, g, w_in_c, w_out_c]
    if ple_args is not None:
        pe2d, g_ple, w_pg, w_ple = ple_args
        in_specs += [pl.BlockSpec((tm, pe2d.shape[1]), row), _const_spec(g_ple.shape), _const_spec(w_pg.shape),
                     _const_spec(w_ple.shape)]
        args += [pe2d, g_ple, w_pg, w_ple]
    return pl.pallas_call(
        functools.partial(_ffn_kernel, n_chunks=n_chunks, ple=ple_args is not None),
        grid=(n // tm,),
        in_specs=in_specs,
        out_specs=pl.BlockSpec((tm, d), row),
        out_shape=jax.ShapeDtypeStruct((n, d), F32),
        scratch_shapes=[pltpu.VMEM((tm, d), F32)],
        compiler_params=_cparams(("parallel",)),
        name="ffn_ple" if ple_args is not None else "ffn",
    )(*args)


_T_CQ = 0
_T_CKV = Q_RANK
_T_KR = Q_RANK + KV_RANK
_T_QC = A_COLS
_T_KC = A_COLS + HC * HD
_T_VC = A_COLS + (HC + KVC) * HD
_T_ROWS = A_COLS + C_COLS


def _proj_kernel(x_ref, gmix_ref, wb_ref, wact_ref, ones_ref, gqb_ref, gkb_ref, gcq_ref, gckv_ref, wuqt_ref,
                 wukvt_ref, gqa_ref, gka_ref, gqc_ref, gkc_ref, tab_ref,
                 qkvb_ref, qat_ref, ka_ref, vat_ref, kna_ref, qct_ref, kc_ref, vct_ref, knc_ref, *, scale_a, scale_c):
    ts = x_ref.shape[0]

    def key_tail(rows):
        r = lax.broadcasted_iota(jnp.int32, (rows, ts), 0)
        return jnp.where(r == rows - 1, 1.0, 0.0).astype(F32)

    def key_norm(kt):
        return jnp.sqrt(jnp.sum(kt * kt, axis=0, keepdims=True))

    ub = _rms_rows(x_ref[...], gmix_ref[...]).astype(BF16)

    pb = jnp.dot(ub, wb_ref[...], preferred_element_type=F32)
    ones = ones_ref[...]
    for part, g_ref in ((0, gqb_ref), (1, gkb_ref), (2, None)):
        for c in range(N_GROUPS_B):
            blk = part * N_GROUPS_B + c
            xx = pb[:, blk * GROUP_B_COLS:(blk + 1) * GROUP_B_COLS]
            if g_ref is not None:
                ms = jnp.dot((xx * xx).astype(BF16), ones, preferred_element_type=F32)
                xx = xx * lax.rsqrt(ms + NORM_EPS) * g_ref[...]
            for half in range(LANE_HALVES):
                qkvb_ref[LANE_HALVES * blk + half] = xx[:, half * LANES:(half + 1) * LANES]

    pt = lax.dot_general(wact_ref[...], ub, (((1,), (1,)), ((), ())), preferred_element_type=F32)
    cos_p, sin_p, cos_r, sin_r, cos_c, sin_c = (tab_ref[i] for i in range(6))
    ones_rows = jnp.ones((V_ROWS - V_A, ts), BF16)

    cq = _rms_cols(pt[_T_CQ:_T_CQ + Q_RANK], gcq_ref[...]).astype(BF16)
    qa = jnp.dot(wuqt_ref[...], cq, preferred_element_type=F32)
    ckv = _rms_cols(pt[_T_CKV:_T_CKV + KV_RANK], gckv_ref[...]).astype(BF16)
    kv = jnp.dot(wukvt_ref[...], ckv, preferred_element_type=F32)
    gqa = gqa_ref[...]
    gka = gka_ref[...]
    kr1, kr2 = _rope_cols(_rms_cols(pt[_T_KR:_T_KR + ROPE_A], gka[NOPE_A:]), cos_p, sin_p)
    k_tail = key_tail(QK_PAD - QK_A)
    kr_sq = jnp.sum(kr1 * kr1 + kr2 * kr2, axis=0, keepdims=True)
    for h in range(HA):
        q0 = h * QK_A
        qn = _rms_cols(qa[q0:q0 + NOPE_A], gqa[:NOPE_A])
        q1, q2 = _rope_cols(_rms_cols(qa[q0 + NOPE_A:q0 + QK_A], gqa[NOPE_A:]), cos_p, sin_p)
        qat_ref[h, 0:NOPE_A] = (qn * scale_a).astype(BF16)
        qat_ref[h, NOPE_A:NOPE_A + ROPE_HALF] = (q1 * scale_a).astype(BF16)
        qat_ref[h, NOPE_A + ROPE_HALF:QK_A] = (q2 * scale_a).astype(BF16)
        qat_ref[h, QK_A:QK_PAD] = jnp.zeros((QK_PAD - QK_A, ts), BF16)
        k0 = h * (NOPE_A + V_A)
        kn = _rms_cols(kv[k0:k0 + NOPE_A], gka[:NOPE_A])
        kt = jnp.concatenate([kn, kr1, kr2, k_tail], axis=0)
        ka_ref[h] = kt.T.astype(BF16)
        kna_ref[h] = jnp.sqrt(jnp.sum(kn * kn, axis=0, keepdims=True) + kr_sq)
        vat_ref[h, 0:V_A] = kv[k0 + NOPE_A:k0 + NOPE_A + V_A].astype(BF16)
        vat_ref[h, V_A:V_ROWS] = ones_rows

    def axial(t):
        a1, a2 = _rope_cols(t[:HD // 2], cos_r, sin_r)
        b1, b2 = _rope_cols(t[HD // 2:], cos_c, sin_c)
        return a1, a2, b1, b2

    gqc = gqc_ref[...]
    gkc = gkc_ref[...]
    for h in range(HC):
        parts = axial(_rms_cols(pt[_T_QC + h * HD:_T_QC + (h + 1) * HD], gqc) * scale_c)
        qct_ref[h, 0:HD] = jnp.concatenate(parts, axis=0).astype(BF16)
        qct_ref[h, HD:QK_PAD] = jnp.zeros((QK_PAD - HD, ts), BF16)
    c_tail = key_tail(QK_PAD - HD)
    for h in range(KVC):
        kh = jnp.concatenate(axial(_rms_cols(pt[_T_KC + h * HD:_T_KC + (h + 1) * HD], gkc)), axis=0)
        kt = jnp.concatenate([kh, c_tail], axis=0)
        kc_ref[h] = kt.T.astype(BF16)
        knc_ref[h] = key_norm(kh)
        vct_ref[h, 0:HD] = pt[_T_VC + h * HD:_T_VC + (h + 1) * HD].astype(BF16)
        vct_ref[h, HD:V_ROWS] = ones_rows


def _proj(x3d, lw, tab, ts=512):
    b, s, d = x3d.shape
    ts = min(ts, s)
    consts = [lw["g_mix"], lw["w_b"], lw["w_act"], lw["ones_blk"], lw["g_qb"], lw["g_kb"], lw["g_cq"], lw["g_ckv"],
              lw["w_uqt"], lw["w_ukvt"], lw["g_qa"], lw["g_ka"], lw["g_qc"], lw["g_kc"]]
    in_specs = ([pl.BlockSpec((None, ts, d), lambda bi, i: (bi, i, 0))] + [_const_spec(c.shape) for c in consts]
                + [pl.BlockSpec((6, ROPE_HALF, ts), lambda bi, i: (0, 0, i))])
    tok_major = lambda h: pl.BlockSpec((None, h, ts, QK_PAD), lambda bi, i: (bi, 0, i, 0))
    feat_major = lambda h, r: pl.BlockSpec((None, h, r, ts), lambda bi, i: (bi, 0, 0, i))
    out_specs = [pl.BlockSpec((None, B_COLS // LANES, ts, LANES), lambda bi, i: (bi, 0, i, 0)),
                 feat_major(HA, QK_PAD), tok_major(HA), feat_major(HA, V_ROWS), feat_major(HA, 1),
                 feat_major(HC, QK_PAD), tok_major(KVC), feat_major(KVC, V_ROWS), feat_major(KVC, 1)]
    out_shape = [jax.ShapeDtypeStruct((b, B_COLS // LANES, s, LANES), F32),
                 jax.ShapeDtypeStruct((b, HA, QK_PAD, s), BF16), jax.ShapeDtypeStruct((b, HA, s, QK_PAD), BF16),
                 jax.ShapeDtypeStruct((b, HA, V_ROWS, s), BF16), jax.ShapeDtypeStruct((b, HA, 1, s), F32),
                 jax.ShapeDtypeStruct((b, HC, QK_PAD, s), BF16), jax.ShapeDtypeStruct((b, KVC, s, QK_PAD), BF16),
                 jax.ShapeDtypeStruct((b, KVC, V_ROWS, s), BF16), jax.ShapeDtypeStruct((b, KVC, 1, s), F32)]
    return pl.pallas_call(
        functools.partial(_proj_kernel, scale_a=QK_A ** -0.5 * LOG2E, scale_c=HD ** -0.5 * LOG2E),
        grid=(b, s // ts),
        in_specs=in_specs,
        out_specs=out_specs,
        out_shape=out_shape,
        compiler_params=_cparams(("parallel", "parallel")),
        name="proj",
    )(x3d, *consts, tab)


def _attn_kernel(kmax_ref, qt_ref, k_ref, vt_ref, ot_ref, acc_ref, m_ref, *, tk, n_sub):
    s_len = k_ref.shape[0]
    qt = qt_ref[...]
    qf = qt.astype(F32)
    qn = jnp.sqrt(jnp.sum(qf * qf, axis=0, keepdims=True))
    bound = qn * kmax_ref[0:1, 0:1] * (1.0 + 2.0 ** -6) + 1.0
    row = lax.broadcasted_iota(jnp.int32, qt.shape, 0)
    qa = jnp.where(row == QK_PAD - 1, -bound, qf).astype(BF16)

    acc_ref[...] = jnp.zeros_like(acc_ref)
    span = tk * n_sub

    def bounded(jj, carry):
        off = jj * span
        chunk = lambda u: pl.ds(pl.multiple_of(off + u * tk, tk), tk)
        score = lambda u: jnp.dot(k_ref[chunk(u), :], qa, preferred_element_type=F32)
        st = [score(u) for u in range(min(SCORE_LOOKAHEAD, n_sub))]
        tot = None
        for u in range(n_sub):
            if u + SCORE_LOOKAHEAD < n_sub:
                st.append(score(u + SCORE_LOOKAHEAD))
            pv = jnp.dot(vt_ref[:, chunk(u)], jnp.exp2(st[u]).astype(BF16), preferred_element_type=F32)
            tot = pv if tot is None else tot + pv
        acc_ref[...] += tot
        return carry

    lax.fori_loop(0, s_len // span, bounded, 0)
    acc = acc_ref[...]
    denom = acc[V_A:V_A + 1]
    trusted = jnp.min(denom) >= DENOM_FLOOR

    @pl.when(trusted)
    def _():
        ot_ref[...] = (acc[:V_A] / denom).astype(ot_ref.dtype)

    @pl.when(jnp.logical_not(trusted))
    def _():
        m_ref[...] = jnp.full_like(m_ref, NEG_INF)
        acc_ref[...] = jnp.zeros_like(acc_ref)

        def running_max(j, carry):
            o = pl.multiple_of(j * tk, tk)
            st = jnp.dot(k_ref[pl.ds(o, tk), :], qt, preferred_element_type=F32)
            m_old = m_ref[...]
            m_new = jnp.maximum(m_old, jnp.max(st, axis=0, keepdims=True))
            p = jnp.exp2(st - m_new).astype(BF16)
            acc_ref[...] = jnp.exp2(m_old - m_new) * acc_ref[...] + jnp.dot(
                vt_ref[:, pl.ds(o, tk)], p, preferred_element_type=F32)
            m_ref[...] = m_new
            return carry

        lax.fori_loop(0, s_len // tk, running_max, 0)
        a = acc_ref[...]
        ot_ref[...] = (a[:V_A] / a[V_A:V_A + 1]).astype(ot_ref.dtype)


def _attention(qt, k, vt, knorm, tq=512, tk=256, n_sub=16):
    b, hq, _, s = qt.shape
    hk = k.shape[1]
    rep = hq // hk
    tq = min(tq, s)
    tk = min(tk, s)
    n_sub = min(n_sub, s // tk)
    kmax = jnp.broadcast_to(jnp.max(knorm, axis=(2, 3))[:, :, None, None], (b, hk, 8, 128))
    return pl.pallas_call(
        functools.partial(_attn_kernel, tk=tk, n_sub=n_sub),
        grid=(b, hq, s // tq),
        in_specs=[pl.BlockSpec((None, None, 8, 128), lambda bi, h, i: (bi, h // rep, 0, 0)),
                  pl.BlockSpec((None, None, QK_PAD, tq), lambda bi, h, i: (bi, h, 0, i)),
                  pl.BlockSpec((None, None, s, QK_PAD), lambda bi, h, i: (bi, h // rep, 0, 0)),
                  pl.BlockSpec((None, None, V_ROWS, s), lambda bi, h, i: (bi, h // rep, 0, 0))],
        out_specs=pl.BlockSpec((None, None, V_A, tq), lambda bi, h, i: (bi, h, 0, i)),
        out_shape=jax.ShapeDtypeStruct((b, hq, V_A, s), BF16),
        scratch_shapes=[pltpu.VMEM((V_ROWS, tq), F32), pltpu.VMEM((1, tq), F32)],
        compiler_params=_cparams(("parallel", "parallel", "arbitrary")),
        name="attention",
    )(kmax, qt, k, vt)


def _dilated_kernel(*refs, n_t):
    n_in = 7 * N_GROUPS_B
    bias_ref, o_ref, kbuf, vbuf, os_ref, ls_ref = refs[n_in:]
    tile = o_ref.shape[0]
    rad = BAND_R
    ti = pl.program_id(1)
    col = lax.broadcasted_iota(jnp.int32, (1, 3 * rad), 1)
    edge_lo = jnp.where((col < rad) & (ti == 0), NEG_INF, 0.0)
    edge_hi = jnp.where((col >= 2 * rad) & (ti == n_t - 1), NEG_INF, 0.0)
    head_of_lane = lax.broadcasted_iota(jnp.int32, (1, GROUP_B_COLS), 1) // HD
    for g, (_, dil) in enumerate(DIL_PAIRS):
        q_ref, kp_ref, kc_ref, kn_ref, vp_ref, vc_ref, vn_ref = refs[7 * g:7 * g + 7]
        lblk = tile // dil
        n_blk = lblk // rad

        def class_rows(r, start, size):
            return pl.ds(start, size) if dil == 1 else pl.ds(r + dil * start, size, stride=dil)

        def rows(ref, r, start, size):
            sl = class_rows(r, start, size)
            return jnp.concatenate([ref[half, sl, :] for half in range(LANE_HALVES)], axis=1)

        for r in range(dil):
            for buf, p_ref, c_ref, n_ref in ((kbuf, kp_ref, kc_ref, kn_ref), (vbuf, vp_ref, vc_ref, vn_ref)):
                buf[0:rad] = rows(p_ref, r, 0, rad).astype(BF16)
                buf[rad:rad + lblk] = rows(c_ref, r, 0, lblk).astype(BF16)
                buf[rad + lblk:2 * rad + lblk] = rows(n_ref, r, 0, rad).astype(BF16)
            for c in range(n_blk):
                q = rows(q_ref, r, c * rad, rad).astype(BF16)
                kw = kbuf[c * rad:(c + 3) * rad, :]
                vw = vbuf[c * rad:(c + 3) * rad, :]
                qs = jnp.concatenate([jnp.where(head_of_lane == h, q, jnp.zeros_like(q)) for h in range(HPG_B)],
                                     axis=0)
                s = lax.dot_general(qs, kw, (((1,), (1,)), ((), ())), preferred_element_type=F32)
                s = s * (HD ** -0.5) + bias_ref[g]
                if c == 0:
                    s = s + edge_lo
                if c == n_blk - 1:
                    s = s + edge_hi
                m = jnp.max(s, axis=-1, keepdims=True)
                p = jnp.exp(s - m)
                l = jnp.sum(p, axis=-1, keepdims=True)
                ov = jnp.dot((p / l).astype(BF16), vw, preferred_element_type=F32)
                lse = m + jnp.log(l)
                o_acc = jnp.zeros((rad, GROUP_B_COLS), F32)
                lse_acc = jnp.zeros((rad, GROUP_B_COLS), F32)
                for h in range(HPG_B):
                    sel = head_of_lane == h
                    o_acc = o_acc + jnp.where(sel, ov[h * rad:(h + 1) * rad], 0.0)
                    lse_acc = lse_acc + jnp.where(sel, lse[h * rad:(h + 1) * rad], 0.0)
                dst = class_rows(r, c * rad, rad)
                for half in range(LANE_HALVES):
                    os_ref[g, half, dst, :] = o_acc[:, half * LANES:(half + 1) * LANES]
                    ls_ref[g, half, dst, :] = lse_acc[:, half * LANES:(half + 1) * LANES]

    for half in range(LANE_HALVES):
        l0, l1, l2 = ls_ref[0, half], ls_ref[1, half], ls_ref[2, half]
        m = jnp.maximum(jnp.maximum(l0, l1), l2)
        e0, e1, e2 = jnp.exp(l0 - m), jnp.exp(l1 - m), jnp.exp(l2 - m)
        inv = 1.0 / (e0 + e1 + e2)
        o_ref[:, half * LANES:(half + 1) * LANES] = (
            (e0 * inv) * os_ref[0, half] + (e1 * inv) * os_ref[1, half] + (e2 * inv) * os_ref[2, half]
        ).astype(o_ref.dtype)


def _dilated(qkvb, bias):
    b, _, s, _ = qkvb.shape
    tile = DIL_TILE
    n_t = s // tile
    in_specs = []
    for g, (_, dil) in enumerate(DIL_PAIRS):
        halo = BAND_R * dil
        per_tile = tile // halo
        n_halo = s // halo

        def cur(part, g=g):
            return pl.BlockSpec((None, LANE_HALVES, tile, LANES), lambda bi, i: (bi, part * N_GROUPS_B + g, i, 0))

        def prev(part, g=g, halo=halo, per_tile=per_tile):
            return pl.BlockSpec((None, LANE_HALVES, halo, LANES),
                                lambda bi, i: (bi, part * N_GROUPS_B + g, jnp.maximum(i * per_tile - 1, 0), 0))

        def nxt(part, g=g, halo=halo, per_tile=per_tile, n_halo=n_halo):
            return pl.BlockSpec((None, LANE_HALVES, halo, LANES),
                                lambda bi, i: (bi, part * N_GROUPS_B + g, jnp.minimum((i + 1) * per_tile, n_halo - 1), 0))

        in_specs += [cur(0), prev(1), cur(1), nxt(1), prev(2), cur(2), nxt(2)]
    in_specs.append(pl.BlockSpec(bias.shape, lambda bi, i: (0, 0, 0)))
    return pl.pallas_call(
        functools.partial(_dilated_kernel, n_t=n_t),
        grid=(b, n_t),
        in_specs=in_specs,
        out_specs=pl.BlockSpec((None, tile, GROUP_B_COLS), lambda bi, i: (bi, i, 0)),
        out_shape=jax.ShapeDtypeStruct((b, s, GROUP_B_COLS), BF16),
        scratch_shapes=[pltpu.VMEM((tile + 2 * BAND_R, GROUP_B_COLS), BF16),
                        pltpu.VMEM((tile + 2 * BAND_R, GROUP_B_COLS), BF16),
                        pltpu.VMEM((N_GROUPS_B, LANE_HALVES, tile, LANES), F32),
                        pltpu.VMEM((N_GROUPS_B, LANE_HALVES, tile, LANES), F32)],
        compiler_params=_cparams(("parallel", "parallel")),
        name="dilated",
    )(*([qkvb] * (7 * N_GROUPS_B)), bias)


def _merge_kernel(x_ref, gmix_ref, oat_ref, oct_ref, ob_ref, wg_ref, bg_ref, woa_ref, wob_ref, woc_ref, wout_ref,
                  o_ref):
    x = x_ref[...]
    ub = _rms_rows(x, gmix_ref[...]).astype(BF16)
    d = x.shape[1]
    tn = (((0,), (0,)), ((), ()))
    ya = lax.dot_general(oat_ref[...], woa_ref[...], tn, preferred_element_type=F32)
    yb = jnp.dot(ob_ref[...], wob_ref[...], preferred_element_type=F32)
    yc = lax.dot_general(oct_ref[...], woc_ref[...], tn, preferred_element_type=F32)

    def gate(n):
        z = jnp.dot(ub, wg_ref[:, n * d:(n + 1) * d], preferred_element_type=F32) + bg_ref[:, n * d:(n + 1) * d]
        return jax.nn.sigmoid(z)

    merged = gate(0) * ya + gate(1) * yb + gate(2) * yc
    o_ref[...] = x + jnp.dot(merged.astype(BF16), wout_ref[...], preferred_element_type=F32)


def _merge(x3d, oat, oct, ob, lw, ts=512):
    b, s, d = x3d.shape
    ts = min(ts, s)
    tok = lambda w: pl.BlockSpec((None, ts, w), lambda bi, i: (bi, i, 0))
    feat = lambda r: pl.BlockSpec((None, r, ts), lambda bi, i: (bi, 0, i))
    consts = [lw["w_gate"], lw["b_gate"], lw["w_oa"], lw["w_ob"], lw["w_oc"], lw["w_out"]]
    return pl.pallas_call(
        _merge_kernel,
        grid=(b, s // ts),
        in_specs=[tok(d), _const_spec(lw["g_mix"].shape), feat(HA * V_A), feat(HC * HD), tok(GROUP_B_COLS)]
                 + [_const_spec(c.shape) for c in consts],
        out_specs=tok(d),
        out_shape=jax.ShapeDtypeStruct((b, s, d), F32),
        compiler_params=_cparams(("parallel", "parallel")),
        name="merge",
    )(x3d, lw["g_mix"], oat.reshape(b, HA * V_A, s), oct.reshape(b, HC * HD, s), ob, *consts)


def _t5_bucket(rel):
    nb = T5_BUCKETS // 2
    max_exact = nb // 2
    n = np.abs(rel)
    large = max_exact + (np.log(np.maximum(n, 1) / max_exact) / math.log(T5_MAX_DIST / max_exact)
                         * (nb - max_exact)).astype(np.int32)
    large = np.minimum(large, nb - 1)
    return (rel > 0).astype(np.int32) * nb + np.where(n < max_exact, n, large).astype(np.int32)


def _band_bias(rel_bias):
    qi = np.arange(BAND_R)[:, None]
    kj = np.arange(3 * BAND_R)[None, :]
    rel = kj - BAND_R - qi
    band = np.where(np.abs(rel) <= BAND_R, 0.0, NEG_INF).astype(np.float32)
    out = []
    for g, (_, dil) in enumerate(DIL_PAIRS):
        tab = rel_bias[:, g * HPG_B:(g + 1) * HPG_B]
        bias = jnp.transpose(tab[_t5_bucket(rel * dil)], (2, 0, 1)).astype(F32)
        out.append((bias + band[None]).reshape(HPG_B * BAND_R, 3 * BAND_R))
    return jnp.stack(out, axis=0)


def _rope_tables(s):
    freqs = (ROPE_THETA ** (-np.arange(ROPE_HALF) / ROPE_HALF)).astype(np.float32)
    pos = jnp.arange(s, dtype=jnp.int32)
    rows = jnp.repeat(jnp.arange(s // GRID_W, dtype=jnp.int32), GRID_W)
    cols = pos % GRID_W
    tabs = []
    for p in (pos, rows, cols):
        ang = p.astype(F32)[:, None] * jnp.asarray(freqs)[None, :]
        tabs += [jnp.cos(ang).T, jnp.sin(ang).T]
    return jnp.stack(tabs, axis=0)


def _ffn_weights(w_in, w_out):
    d = w_in.shape[0]
    n_chunks = D_FF // FFN_CHUNK
    w_in_c = w_in.astype(BF16).reshape(d, 2 * n_chunks, FFN_CHUNK).transpose(1, 0, 2)
    w_out_c = w_out.astype(BF16).reshape(n_chunks, FFN_CHUNK, d)
    return w_in_c, w_out_c


def _layer_weights(W, i):
    row = lambda v: v.reshape(1, -1).astype(F32)
    colv = lambda v: v.reshape(-1, 1).astype(F32)
    w_in = W["w_in"][i]
    w_a = w_in[:, :A_COLS]
    w_b = w_in[:, A_COLS:A_COLS + B_COLS]
    w_c = w_in[:, A_COLS + B_COLS:]
    head = np.arange(GROUP_B_COLS) // HD
    ones_blk = jnp.asarray((head[:, None] == head[None, :]).astype(np.float32) / HD, dtype=BF16)
    lw = dict(
        g_ffn1=row(W["g_ffn1"][i]), g_ffn2=row(W["g_ffn2"][i]), g_mix=row(W["g_mix"][i]), g_ple=row(W["g_ple"][i]),
        w_b=w_b.astype(BF16),
        w_act=jnp.concatenate([w_a, w_c], axis=1).T.astype(BF16),
        ones_blk=ones_blk,
        g_qb=row(jnp.tile(W["g_qb"][i], HPG_B)), g_kb=row(jnp.tile(W["g_kb"][i], HPG_B)),
        g_cq=colv(W["g_cq"][i]), g_ckv=colv(W["g_ckv"][i]),
        w_uqt=W["w_uq"][i].T.astype(BF16), w_ukvt=W["w_ukv"][i].T.astype(BF16),
        g_qa=colv(W["g_qa"][i]), g_ka=colv(W["g_ka"][i]), g_qc=colv(W["g_qc"][i]), g_kc=colv(W["g_kc"][i]),
        w_gate=W["w_gate"][i].astype(BF16), b_gate=row(W["b_gate"][i]),
        w_oa=W["w_oa"][i].astype(BF16), w_ob=W["w_ob"][i].astype(BF16), w_oc=W["w_oc"][i].astype(BF16),
        w_out=W["w_out"][i].astype(BF16),
        w_pg=W["w_pg"][i].astype(BF16), w_ple=W["w_ple"][i].astype(BF16),
    )
    lw["ffn1"] = _ffn_weights(W["w_ffn1_in"][i], W["w_ffn1_out"][i])
    lw["ffn2"] = _ffn_weights(W["w_ffn2_in"][i], W["w_ffn2_out"][i])
    return lw


def _trunk(x, p, layers, bias):
    b, s, d = x.shape
    tab = _rope_tables(s)
    for i, lw in enumerate(layers):
        x1 = _ffn(x.reshape(b * s, d), lw["g_ffn1"], *lw["ffn1"]).reshape(b, s, d)
        qkvb, qat, ka, vat, kna, qct, kc, vct, knc = _proj(x1, lw, tab)
        oat = _attention(qat, ka, vat, kna)
        oct = _attention(qct, kc, vct, knc)
        x2 = _merge(x1, oat, oct, _dilated(qkvb, bias), lw)
        x = _ffn(x2.reshape(b * s, d), lw["g_ffn2"], *lw["ffn2"],
                 ple_args=(p[i].reshape(b * s, PLE_DIM), lw["g_ple"], lw["w_pg"], lw["w_ple"])).reshape(b, s, d)
    return x


def kernel(x_prompt, x_sample, p_prompt, p_sample, g_ffn1, w_ffn1_in, w_ffn1_out, g_mix, w_in, g_cq, g_ckv, w_uq,
           w_ukv, g_qa, g_ka, g_qb, g_kb, rel_bias, g_qc, g_kc, w_gate, b_gate, w_oa, w_ob, w_oc, w_out, g_ffn2,
           w_ffn2_in, w_ffn2_out, g_ple, w_pg, w_ple):
    W = dict(g_ffn1=g_ffn1, w_ffn1_in=w_ffn1_in, w_ffn1_out=w_ffn1_out, g_mix=g_mix, w_in=w_in,
             g_cq=g_cq, g_ckv=g_ckv, w_uq=w_uq, w_ukv=w_ukv, g_qa=g_qa, g_ka=g_ka,
             g_qb=g_qb, g_kb=g_kb, g_qc=g_qc, g_kc=g_kc,
             w_gate=w_gate, b_gate=b_gate, w_oa=w_oa, w_ob=w_ob, w_oc=w_oc, w_out=w_out,
             g_ffn2=g_ffn2, w_ffn2_in=w_ffn2_in, w_ffn2_out=w_ffn2_out,
             g_ple=g_ple, w_pg=w_pg, w_ple=w_ple)
    layers = [_layer_weights(W, i) for i in range(g_mix.shape[0])]
    bias = _band_bias(rel_bias)
    return (_trunk(x_prompt, p_prompt, layers, bias), _trunk(x_sample, p_sample, layers, bias))
```

```python
import functools
import math

import numpy as np
import jax
import jax.numpy as jnp
from jax import lax
from jax.experimental import pallas as pl
from jax.experimental.pallas import tpu as pltpu

F32 = jnp.float32
BF16 = jnp.bfloat16

D_MODEL = 1024
GRID_W = 64
PLE_DIM = 256
D_FF = 2816
NORM_EPS = 1e-6
ROPE_THETA = 10000.0
NEG_INF = -1e30

HA = 8
Q_RANK = 256
KV_RANK = 128
NOPE_A = 64
ROPE_A = 32
V_A = 64
QK_A = NOPE_A + ROPE_A

HD = 64
DIL_PAIRS = ((128, 1), (512, 4), (2048, 16))
N_GROUPS_B = 3
HPG_B = 4
N_HEADS_B = N_GROUPS_B * HPG_B
T5_BUCKETS = 32
T5_MAX_DIST = 1024
BAND_R = 64
DIL_TILE = BAND_R * max(d for _, d in DIL_PAIRS)

HC = 8
KVC = 2
GC = HC // KVC

A_COLS = Q_RANK + KV_RANK + ROPE_A
B_COLS = 3 * N_HEADS_B * HD
C_COLS = (HC + 2 * KVC) * HD
GROUP_B_COLS = HPG_B * HD
LANES = 128
LANE_HALVES = GROUP_B_COLS // LANES

ROPE_HALF = 16
QK_PAD = 128
LOG2E = math.log2(math.e)

VMEM_LIMIT_BYTES = 56 * 1024 * 1024
FFN_CHUNK = 256
SCORE_LOOKAHEAD = 2
DENOM_FLOOR = 2.0 ** -80


def _cparams(semantics):
    return pltpu.CompilerParams(dimension_semantics=semantics, vmem_limit_bytes=VMEM_LIMIT_BYTES)


def _const_spec(shape):
    nd = len(shape)
    return pl.BlockSpec(shape, lambda *_: (0,) * nd, pipeline_mode=pl.Buffered(1))


def _rms_rows(x, g):
    ms = jnp.mean(x * x, axis=-1, keepdims=True)
    return x * lax.rsqrt(ms + NORM_EPS) * g


def _rms_cols(x, g):
    ms = jnp.mean(x * x, axis=0, keepdims=True)
    return x * lax.rsqrt(ms + NORM_EPS) * g


def _rope_cols(x, cos, sin):
    x1, x2 = x[:ROPE_HALF], x[ROPE_HALF:]
    return x1 * cos - x2 * sin, x1 * sin + x2 * cos


def _ffn_kernel(*refs, n_chunks, ple):
    if ple:
        x_ref, g_ref, win_ref, wout_ref, pe_ref, gple_ref, wpg_ref, wple_ref, o_ref, acc_ref = refs
    else:
        x_ref, g_ref, win_ref, wout_ref, o_ref, acc_ref = refs
    x = x_ref[...]
    u = _rms_rows(x, g_ref[...]).astype(BF16)
    acc_ref[...] = jnp.zeros_like(acc_ref)

    def body(c, carry):
        a = jnp.dot(u, win_ref[c], preferred_element_type=F32)
        b = jnp.dot(u, win_ref[n_chunks + c], preferred_element_type=F32)
        h = (a * jax.nn.sigmoid(a) * b).astype(BF16)
        acc_ref[...] += jnp.dot(h, wout_ref[c], preferred_element_type=F32)
        return carry

    lax.fori_loop(0, n_chunks, body, 0)
    y = x + 0.5 * acc_ref[...]
    if ple:
        gate = jax.nn.sigmoid(
            jnp.dot(_rms_rows(y, gple_ref[...]).astype(BF16), wpg_ref[...], preferred_element_type=F32))
        y = y + gate * jnp.dot(pe_ref[...].astype(BF16), wple_ref[...], preferred_element_type=F32)
    o_ref[...] = y


def _ffn(x2d, g, w_in_c, w_out_c, ple_args=None, tm=1024):
    n, d = x2d.shape
    n_chunks = w_out_c.shape[0]
    tm = min(tm, n)
    row = lambda i: (i, 0)
    in_specs = [pl.BlockSpec((tm, d), row), _const_spec(g.shape), _const_spec(w_in_c.shape), _const_spec(w_out_c.shape)]
    args = [x2d, g, w_in_c, w_out_c]
    if ple_args is not None:
        pe2d, g_ple, w_pg, w_ple = ple_args
        in_specs += [pl.BlockSpec((tm, pe2d.shape[1]), row), _const_spec(g_ple.shape), _const_spec(w_pg.shape),
                     _const_spec(w_ple.shape)]
        args += [pe2d, g_ple, w_pg, w_ple]
    return pl.pallas_call(
        functools.partial(_ffn_kernel, n_chunks=n_chunks, ple=ple_args is not None),
        grid=(n // tm,),
        in_specs=in_specs,
        out_specs=pl.BlockSpec((tm, d), row),
        out_shape=jax.ShapeDtypeStruct((n, d), F32),
        scratch_shapes=[pltpu.VMEM((tm, d), F32)],
        compiler_params=_cparams(("parallel",)),
        name="ffn_ple" if ple_args is not None else "ffn",
    )(*args)


_T_CQ = 0
_T_CKV = Q_RANK
_T_KR = Q_RANK + KV_RANK
_T_QC = A_COLS
_T_KC = A_COLS + HC * HD
_T_VC = A_COLS + (HC + KVC) * HD
_T_ROWS = A_COLS + C_COLS


def _proj_kernel(x_ref, gmix_ref, wb_ref, wact_ref, ones_ref, gqb_ref, gkb_ref, gcq_ref, gckv_ref, wuqt_ref,
                 wukvt_ref, gqa_ref, gka_ref, gqc_ref, gkc_ref, tab_ref,
                 qkvb_ref, qat_ref, ka_ref, vat_ref, kna_ref, qct_ref, kc_ref, vct_ref, knc_ref, *, scale_a, scale_c):
    ts = x_ref.shape[0]

    def key_tail(rows):
        r = lax.broadcasted_iota(jnp.int32, (rows, ts), 0)
        return jnp.where(r == rows - 1, 1.0, 0.0).astype(F32)

    def key_norm(kt):
        return jnp.sqrt(jnp.sum(kt * kt, axis=0, keepdims=True))

    ub = _rms_rows(x_ref[...], gmix_ref[...]).astype(BF16)

    pb = jnp.dot(ub, wb_ref[...], preferred_element_type=F32)
    ones = ones_ref[...]
    for part, g_ref in ((0, gqb_ref), (1, gkb_ref), (2, None)):
        for c in range(N_GROUPS_B):
            blk = part * N_GROUPS_B + c
            xx = pb[:, blk * GROUP_B_COLS:(blk + 1) * GROUP_B_COLS]
            if g_ref is not None:
                ms = jnp.dot((xx * xx).astype(BF16), ones, preferred_element_type=F32)
                xx = xx * lax.rsqrt(ms + NORM_EPS) * g_ref[...]
            for half in range(LANE_HALVES):
                qkvb_ref[LANE_HALVES * blk + half] = xx[:, half * LANES:(half + 1) * LANES]

    pt = lax.dot_general(wact_ref[...], ub, (((1,), (1,)), ((), ())), preferred_element_type=F32)
    cos_p, sin_p, cos_r, sin_r, cos_c, sin_c = (tab_ref[i] for i in range(6))

    cq = _rms_cols(pt[_T_CQ:_T_CQ + Q_RANK], gcq_ref[...]).astype(BF16)
    qa = jnp.dot(wuqt_ref[...], cq, preferred_element_type=F32)
    ckv = _rms_cols(pt[_T_CKV:_T_CKV + KV_RANK], gckv_ref[...]).astype(BF16)
    kv = jnp.dot(wukvt_ref[...], ckv, preferred_element_type=F32)
    gqa = gqa_ref[...]
    gka = gka_ref[...]
    kr1, kr2 = _rope_cols(_rms_cols(pt[_T_KR:_T_KR + ROPE_A], gka[NOPE_A:]), cos_p, sin_p)
    k_tail = key_tail(QK_PAD - QK_A)
    kr_sq = jnp.sum(kr1 * kr1 + kr2 * kr2, axis=0, keepdims=True)
    for h in range(HA):
        q0 = h * QK_A
        qn = _rms_cols(qa[q0:q0 + NOPE_A], gqa[:NOPE_A])
        q1, q2 = _rope_cols(_rms_cols(qa[q0 + NOPE_A:q0 + QK_A], gqa[NOPE_A:]), cos_p, sin_p)
        qat_ref[h, 0:NOPE_A] = (qn * scale_a).astype(BF16)
        qat_ref[h, NOPE_A:NOPE_A + ROPE_HALF] = (q1 * scale_a).astype(BF16)
        qat_ref[h, NOPE_A + ROPE_HALF:QK_A] = (q2 * scale_a).astype(BF16)
        qat_ref[h, QK_A:QK_PAD] = jnp.zeros((QK_PAD - QK_A, ts), BF16)
        k0 = h * (NOPE_A + V_A)
        kn = _rms_cols(kv[k0:k0 + NOPE_A], gka[:NOPE_A])
        kt = jnp.concatenate([kn, kr1, kr2, k_tail], axis=0)
        ka_ref[h] = kt.T.astype(BF16)
        kna_ref[h] = jnp.sqrt(jnp.sum(kn * kn, axis=0, keepdims=True) + kr_sq)
        vat_ref[h] = kv[k0 + NOPE_A:k0 + NOPE_A + V_A].astype(BF16)

    def axial(t):
        a1, a2 = _rope_cols(t[:HD // 2], cos_r, sin_r)
        b1, b2 = _rope_cols(t[HD // 2:], cos_c, sin_c)
        return a1, a2, b1, b2

    gqc = gqc_ref[...]
    gkc = gkc_ref[...]
    for h in range(HC):
        parts = axial(_rms_cols(pt[_T_QC + h * HD:_T_QC + (h + 1) * HD], gqc) * scale_c)
        qct_ref[h, 0:HD] = jnp.concatenate(parts, axis=0).astype(BF16)
        qct_ref[h, HD:QK_PAD] = jnp.zeros((QK_PAD - HD, ts), BF16)
    c_tail = key_tail(QK_PAD - HD)
    for h in range(KVC):
        kh = jnp.concatenate(axial(_rms_cols(pt[_T_KC + h * HD:_T_KC + (h + 1) * HD], gkc)), axis=0)
        kt = jnp.concatenate([kh, c_tail], axis=0)
        kc_ref[h] = kt.T.astype(BF16)
        knc_ref[h] = key_norm(kh)
        vct_ref[h] = pt[_T_VC + h * HD:_T_VC + (h + 1) * HD].astype(BF16)


def _proj(x3d, lw, tab, ts=512):
    b, s, d = x3d.shape
    ts = min(ts, s)
    consts = [lw["g_mix"], lw["w_b"], lw["w_act"], lw["ones_blk"], lw["g_qb"], lw["g_kb"], lw["g_cq"], lw["g_ckv"],
              lw["w_uqt"], lw["w_ukvt"], lw["g_qa"], lw["g_ka"], lw["g_qc"], lw["g_kc"]]
    in_specs = ([pl.BlockSpec((None, ts, d), lambda bi, i: (bi, i, 0))] + [_const_spec(c.shape) for c in consts]
                + [pl.BlockSpec((6, ROPE_HALF, ts), lambda bi, i: (0, 0, i))])
    tok_major = lambda h: pl.BlockSpec((None, h, ts, QK_PAD), lambda bi, i: (bi, 0, i, 0))
    feat_major = lambda h, r: pl.BlockSpec((None, h, r, ts), lambda bi, i: (bi, 0, 0, i))
    out_specs = [pl.BlockSpec((None, B_COLS // LANES, ts, LANES), lambda bi, i: (bi, 0, i, 0)),
                 feat_major(HA, QK_PAD), tok_major(HA), feat_major(HA, V_A), feat_major(HA, 1),
                 feat_major(HC, QK_PAD), tok_major(KVC), feat_major(KVC, HD), feat_major(KVC, 1)]
    out_shape = [jax.ShapeDtypeStruct((b, B_COLS // LANES, s, LANES), F32),
                 jax.ShapeDtypeStruct((b, HA, QK_PAD, s), BF16), jax.ShapeDtypeStruct((b, HA, s, QK_PAD), BF16),
                 jax.ShapeDtypeStruct((b, HA, V_A, s), BF16), jax.ShapeDtypeStruct((b, HA, 1, s), F32),
                 jax.ShapeDtypeStruct((b, HC, QK_PAD, s), BF16), jax.ShapeDtypeStruct((b, KVC, s, QK_PAD), BF16),
                 jax.ShapeDtypeStruct((b, KVC, HD, s), BF16), jax.ShapeDtypeStruct((b, KVC, 1, s), F32)]
    return pl.pallas_call(
        functools.partial(_proj_kernel, scale_a=QK_A ** -0.5 * LOG2E, scale_c=HD ** -0.5 * LOG2E),
        grid=(b, s // ts),
        in_specs=in_specs,
        out_specs=out_specs,
        out_shape=out_shape,
        compiler_params=_cparams(("parallel", "parallel")),
        name="proj",
    )(x3d, *consts, tab)


def _attn_kernel(kmax_ref, qt_ref, k_ref, vt_ref, ot_ref, acc_ref, den_ref, m_ref, *, tk, n_sub):
    s_len = k_ref.shape[0]
    qt = qt_ref[...]
    qf = qt.astype(F32)
    qn = jnp.sqrt(jnp.sum(qf * qf, axis=0, keepdims=True))
    bound = qn * kmax_ref[0:1, 0:1] * (1.0 + 2.0 ** -6) + 1.0
    row = lax.broadcasted_iota(jnp.int32, qt.shape, 0)
    qa = jnp.where(row == QK_PAD - 1, -bound, qf).astype(BF16)

    acc_ref[...] = jnp.zeros_like(acc_ref)
    den_ref[...] = jnp.zeros_like(den_ref)
    span = tk * n_sub

    def bounded(jj, carry):
        off = jj * span
        chunk = lambda u: pl.ds(pl.multiple_of(off + u * tk, tk), tk)
        score = lambda u: jnp.dot(k_ref[chunk(u), :], qa, preferred_element_type=F32)
        st = [score(u) for u in range(min(SCORE_LOOKAHEAD, n_sub))]
        tot = den = None
        for u in range(n_sub):
            if u + SCORE_LOOKAHEAD < n_sub:
                st.append(score(u + SCORE_LOOKAHEAD))
            p = jnp.exp2(st[u])
            pv = jnp.dot(vt_ref[:, chunk(u)], p.astype(BF16), preferred_element_type=F32)
            ps = jnp.sum(p, axis=0, keepdims=True)
            tot = pv if tot is None else tot + pv
            den = ps if den is None else den + ps
        acc_ref[...] += tot
        den_ref[...] += den
        return carry

    lax.fori_loop(0, s_len // span, bounded, 0)
    trusted = jnp.min(den_ref[...]) >= DENOM_FLOOR

    @pl.when(jnp.logical_not(trusted))
    def _():
        m_ref[...] = jnp.full_like(m_ref, NEG_INF)
        acc_ref[...] = jnp.zeros_like(acc_ref)
        den_ref[...] = jnp.zeros_like(den_ref)

        def running_max(j, carry):
            o = pl.multiple_of(j * tk, tk)
            st = jnp.dot(k_ref[pl.ds(o, tk), :], qt, preferred_element_type=F32)
            m_old = m_ref[...]
            m_new = jnp.maximum(m_old, jnp.max(st, axis=0, keepdims=True))
            alpha = jnp.exp2(m_old - m_new)
            p = jnp.exp2(st - m_new)
            acc_ref[...] = alpha * acc_ref[...] + jnp.dot(
                vt_ref[:, pl.ds(o, tk)], p.astype(BF16), preferred_element_type=F32)
            den_ref[...] = alpha * den_ref[...] + jnp.sum(p, axis=0, keepdims=True)
            m_ref[...] = m_new
            return carry

        lax.fori_loop(0, s_len // tk, running_max, 0)

    ot_ref[...] = (acc_ref[...] / den_ref[...]).astype(ot_ref.dtype)


def _attention(qt, k, vt, knorm, tq=512, tk=256, n_sub=64):
    b, hq, _, s = qt.shape
    hk = k.shape[1]
    rep = hq // hk
    tq = min(tq, s)
    tk = min(tk, s)
    n_sub = min(n_sub, s // tk)
    kmax = jnp.broadcast_to(jnp.max(knorm, axis=(2, 3))[:, :, None, None], (b, hk, 8, 128))
    return pl.pallas_call(
        functools.partial(_attn_kernel, tk=tk, n_sub=n_sub),
        grid=(b, hq, s // tq),
        in_specs=[pl.BlockSpec((None, None, 8, 128), lambda bi, h, i: (bi, h // rep, 0, 0)),
                  pl.BlockSpec((None, None, QK_PAD, tq), lambda bi, h, i: (bi, h, 0, i)),
                  pl.BlockSpec((None, None, s, QK_PAD), lambda bi, h, i: (bi, h // rep, 0, 0)),
                  pl.BlockSpec((None, None, V_A, s), lambda bi, h, i: (bi, h // rep, 0, 0))],
        out_specs=pl.BlockSpec((None, None, V_A, tq), lambda bi, h, i: (bi, h, 0, i)),
        out_shape=jax.ShapeDtypeStruct((b, hq, V_A, s), BF16),
        scratch_shapes=[pltpu.VMEM((V_A, tq), F32), pltpu.VMEM((1, tq), F32), pltpu.VMEM((1, tq), F32)],
        compiler_params=_cparams(("parallel", "parallel", "arbitrary")),
        name="attention",
    )(kmax, qt, k, vt)


def _dilated_kernel(*refs, n_t):
    n_in = 7 * N_GROUPS_B
    bias_ref, o_ref, kbuf, vbuf, os_ref, ls_ref = refs[n_in:]
    tile = o_ref.shape[0]
    rad = BAND_R
    ti = pl.program_id(1)
    col = lax.broadcasted_iota(jnp.int32, (1, 3 * rad), 1)
    edge_lo = jnp.where((col < rad) & (ti == 0), NEG_INF, 0.0)
    edge_hi = jnp.where((col >= 2 * rad) & (ti == n_t - 1), NEG_INF, 0.0)
    head_of_lane = lax.broadcasted_iota(jnp.int32, (1, GROUP_B_COLS), 1) // HD
    for g, (_, dil) in enumerate(DIL_PAIRS):
        q_ref, kp_ref, kc_ref, kn_ref, vp_ref, vc_ref, vn_ref = refs[7 * g:7 * g + 7]
        lblk = tile // dil
        n_blk = lblk // rad

        def class_rows(r, start, size):
            return pl.ds(start, size) if dil == 1 else pl.ds(r + dil * start, size, stride=dil)

        def rows(ref, r, start, size):
            sl = class_rows(r, start, size)
            return jnp.concatenate([ref[half, sl, :] for half in range(LANE_HALVES)], axis=1)

        for r in range(dil):
            for buf, p_ref, c_ref, n_ref in ((kbuf, kp_ref, kc_ref, kn_ref), (vbuf, vp_ref, vc_ref, vn_ref)):
                buf[0:rad] = rows(p_ref, r, 0, rad).astype(BF16)
                buf[rad:rad + lblk] = rows(c_ref, r, 0, lblk).astype(BF16)
                buf[rad + lblk:2 * rad + lblk] = rows(n_ref, r, 0, rad).astype(BF16)
            for c in range(n_blk):
                q = rows(q_ref, r, c * rad, rad).astype(BF16)
                kw = kbuf[c * rad:(c + 3) * rad, :]
                vw = vbuf[c * rad:(c + 3) * rad, :]
                qs = jnp.concatenate([jnp.where(head_of_lane == h, q, jnp.zeros_like(q)) for h in range(HPG_B)],
                                     axis=0)
                s = lax.dot_general(qs, kw, (((1,), (1,)), ((), ())), preferred_element_type=F32)
                s = s * (HD ** -0.5) + bias_ref[g]
                if c == 0:
                    s = s + edge_lo
                if c == n_blk - 1:
                    s = s + edge_hi
                m = jnp.max(s, axis=-1, keepdims=True)
                p = jnp.exp(s - m)
                l = jnp.sum(p, axis=-1, keepdims=True)
                ov = jnp.dot((p / l).astype(BF16), vw, preferred_element_type=F32)
                lse = m + jnp.log(l)
                o_acc = jnp.zeros((rad, GROUP_B_COLS), F32)
                lse_acc = jnp.zeros((rad, GROUP_B_COLS), F32)
                for h in range(HPG_B):
                    sel = head_of_lane == h
                    o_acc = o_acc + jnp.where(sel, ov[h * rad:(h + 1) * rad], 0.0)
                    lse_acc = lse_acc + jnp.where(sel, lse[h * rad:(h + 1) * rad], 0.0)
                dst = class_rows(r, c * rad, rad)
                for half in range(LANE_HALVES):
                    os_ref[g, half, dst, :] = o_acc[:, half * LANES:(half + 1) * LANES]
                    ls_ref[g, half, dst, :] = lse_acc[:, half * LANES:(half + 1) * LANES]

    for half in range(LANE_HALVES):
        l0, l1, l2 = ls_ref[0, half], ls_ref[1, half], ls_ref[2, half]
        m = jnp.maximum(jnp.maximum(l0, l1), l2)
        e0, e1, e2 = jnp.exp(l0 - m), jnp.exp(l1 - m), jnp.exp(l2 - m)
        inv = 1.0 / (e0 + e1 + e2)
        o_ref[:, half * LANES:(half + 1) * LANES] = (
            (e0 * inv) * os_ref[0, half] + (e1 * inv) * os_ref[1, half] + (e2 * inv) * os_ref[2, half]
        ).astype(o_ref.dtype)


def _dilated(qkvb, bias):
    b, _, s, _ = qkvb.shape
    tile = DIL_TILE
    n_t = s // tile
    in_specs = []
    for g, (_, dil) in enumerate(DIL_PAIRS):
        halo = BAND_R * dil
        per_tile = tile // halo
        n_halo = s // halo

        def cur(part, g=g):
            return pl.BlockSpec((None, LANE_HALVES, tile, LANES), lambda bi, i: (bi, part * N_GROUPS_B + g, i, 0))

        def prev(part, g=g, halo=halo, per_tile=per_tile):
            return pl.BlockSpec((None, LANE_HALVES, halo, LANES),
                                lambda bi, i: (bi, part * N_GROUPS_B + g, jnp.maximum(i * per_tile - 1, 0), 0))

        def nxt(part, g=g, halo=halo, per_tile=per_tile, n_halo=n_halo):
            return pl.BlockSpec((None, LANE_HALVES, halo, LANES),
                                lambda bi, i: (bi, part * N_GROUPS_B + g, jnp.minimum((i + 1) * per_tile, n_halo - 1), 0))

        in_specs += [cur(0), prev(1), cur(1), nxt(1), prev(2), cur(2), nxt(2)]
    in_specs.append(pl.BlockSpec(bias.shape, lambda bi, i: (0, 0, 0)))
    return pl.pallas_call(
        functools.partial(_dilated_kernel, n_t=n_t),
        grid=(b, n_t),
        in_specs=in_specs,
        out_specs=pl.BlockSpec((None, tile, GROUP_B_COLS), lambda bi, i: (bi, i, 0)),
        out_shape=jax.ShapeDtypeStruct((b, s, GROUP_B_COLS), BF16),
        scratch_shapes=[pltpu.VMEM((tile + 2 * BAND_R, GROUP_B_COLS), BF16),
                        pltpu.VMEM((tile + 2 * BAND_R, GROUP_B_COLS), BF16),
                        pltpu.VMEM((N_GROUPS_B, LANE_HALVES, tile, LANES), F32),
                        pltpu.VMEM((N_GROUPS_B, LANE_HALVES, tile, LANES), F32)],
        compiler_params=_cparams(("parallel", "parallel")),
        name="dilated",
    )(*([qkvb] * (7 * N_GROUPS_B)), bias)


def _merge_kernel(x_ref, gmix_ref, oat_ref, oct_ref, ob_ref, wg_ref, bg_ref, woa_ref, wob_ref, woc_ref, wout_ref,
                  o_ref):
    x = x_ref[...]
    ub = _rms_rows(x, gmix_ref[...]).astype(BF16)
    d = x.shape[1]
    tn = (((0,), (0,)), ((), ()))
    ya = lax.dot_general(oat_ref[...], woa_ref[...], tn, preferred_element_type=F32)
    yb = jnp.dot(ob_ref[...], wob_ref[...], preferred_element_type=F32)
    yc = lax.dot_general(oct_ref[...], woc_ref[...], tn, preferred_element_type=F32)

    def gate(n):
        z = jnp.dot(ub, wg_ref[:, n * d:(n + 1) * d], preferred_element_type=F32) + bg_ref[:, n * d:(n + 1) * d]
        return jax.nn.sigmoid(z)

    merged = gate(0) * ya + gate(1) * yb + gate(2) * yc
    o_ref[...] = x + jnp.dot(merged.astype(BF16), wout_ref[...], preferred_element_type=F32)


def _merge(x3d, oat, oct, ob, lw, ts=512):
    b, s, d = x3d.shape
    ts = min(ts, s)
    tok = lambda w: pl.BlockSpec((None, ts, w), lambda bi, i: (bi, i, 0))
    feat = lambda r: pl.BlockSpec((None, r, ts), lambda bi, i: (bi, 0, i))
    consts = [lw["w_gate"], lw["b_gate"], lw["w_oa"], lw["w_ob"], lw["w_oc"], lw["w_out"]]
    return pl.pallas_call(
        _merge_kernel,
        grid=(b, s // ts),
        in_specs=[tok(d), _const_spec(lw["g_mix"].shape), feat(HA * V_A), feat(HC * HD), tok(GROUP_B_COLS)]
                 + [_const_spec(c.shape) for c in consts],
        out_specs=tok(d),
        out_shape=jax.ShapeDtypeStruct((b, s, d), F32),
        compiler_params=_cparams(("parallel", "parallel")),
        name="merge",
    )(x3d, lw["g_mix"], oat.reshape(b, HA * V_A, s), oct.reshape(b, HC * HD, s), ob, *consts)


def _t5_bucket(rel):
    nb = T5_BUCKETS // 2
    max_exact = nb // 2
    n = np.abs(rel)
    large = max_exact + (np.log(np.maximum(n, 1) / max_exact) / math.log(T5_MAX_DIST / max_exact)
                         * (nb - max_exact)).astype(np.int32)
    large = np.minimum(large, nb - 1)
    return (rel > 0).astype(np.int32) * nb + np.where(n < max_exact, n, large).astype(np.int32)


def _band_bias(rel_bias):
    qi = np.arange(BAND_R)[:, None]
    kj = np.arange(3 * BAND_R)[None, :]
    rel = kj - BAND_R - qi
    band = np.where(np.abs(rel) <= BAND_R, 0.0, NEG_INF).astype(np.float32)
    out = []
    for g, (_, dil) in enumerate(DIL_PAIRS):
        tab = rel_bias[:, g * HPG_B:(g + 1) * HPG_B]
        bias = jnp.transpose(tab[_t5_bucket(rel * dil)], (2, 0, 1)).astype(F32)
        out.append((bias + band[None]).reshape(HPG_B * BAND_R, 3 * BAND_R))
    return jnp.stack(out, axis=0)


def _rope_tables(s):
    freqs = (ROPE_THETA ** (-np.arange(ROPE_HALF) / ROPE_HALF)).astype(np.float32)
    pos = jnp.arange(s, dtype=jnp.int32)
    rows = jnp.repeat(jnp.arange(s // GRID_W, dtype=jnp.int32), GRID_W)
    cols = pos % GRID_W
    tabs = []
    for p in (pos, rows, cols):
        ang = p.astype(F32)[:, None] * jnp.asarray(freqs)[None, :]
        tabs += [jnp.cos(ang).T, jnp.sin(ang).T]
    return jnp.stack(tabs, axis=0)


def _ffn_weights(w_in, w_out):
    d = w_in.shape[0]
    n_chunks = D_FF // FFN_CHUNK
    w_in_c = w_in.astype(BF16).reshape(d, 2 * n_chunks, FFN_CHUNK).transpose(1, 0, 2)
    w_out_c = w_out.astype(BF16).reshape(n_chunks, FFN_CHUNK, d)
    return w_in_c, w_out_c


def _layer_weights(W, i):
    row = lambda v: v.reshape(1, -1).astype(F32)
    colv = lambda v: v.reshape(-1, 1).astype(F32)
    w_in = W["w_in"][i]
    w_a = w_in[:, :A_COLS]
    w_b = w_in[:, A_COLS:A_COLS + B_COLS]
    w_c = w_in[:, A_COLS + B_COLS:]
    head = np.arange(GROUP_B_COLS) // HD
    ones_blk = jnp.asarray((head[:, None] == head[None, :]).astype(np.float32) / HD, dtype=BF16)
    lw = dict(
        g_ffn1=row(W["g_ffn1"][i]), g_ffn2=row(W["g_ffn2"][i]), g_mix=row(W["g_mix"][i]), g_ple=row(W["g_ple"][i]),
        w_b=w_b.astype(BF16),
        w_act=jnp.concatenate([w_a, w_c], axis=1).T.astype(BF16),
        ones_blk=ones_blk,
        g_qb=row(jnp.tile(W["g_qb"][i], HPG_B)), g_kb=row(jnp.tile(W["g_kb"][i], HPG_B)),
        g_cq=colv(W["g_cq"][i]), g_ckv=colv(W["g_ckv"][i]),
        w_uqt=W["w_uq"][i].T.astype(BF16), w_ukvt=W["w_ukv"][i].T.astype(BF16),
        g_qa=colv(W["g_qa"][i]), g_ka=colv(W["g_ka"][i]), g_qc=colv(W["g_qc"][i]), g_kc=colv(W["g_kc"][i]),
        w_gate=W["w_gate"][i].astype(BF16), b_gate=row(W["b_gate"][i]),
        w_oa=W["w_oa"][i].astype(BF16), w_ob=W["w_ob"][i].astype(BF16), w_oc=W["w_oc"][i].astype(BF16),
        w_out=W["w_out"][i].astype(BF16),
        w_pg=W["w_pg"][i].astype(BF16), w_ple=W["w_ple"][i].astype(BF16),
    )
    lw["ffn1"] = _ffn_weights(W["w_ffn1_in"][i], W["w_ffn1_out"][i])
    lw["ffn2"] = _ffn_weights(W["w_ffn2_in"][i], W["w_ffn2_out"][i])
    return lw


def _trunk(x, p, layers, bias):
    b, s, d = x.shape
    tab = _rope_tables(s)
    for i, lw in enumerate(layers):
        x1 = _ffn(x.reshape(b * s, d), lw["g_ffn1"], *lw["ffn1"]).reshape(b, s, d)
        qkvb, qat, ka, vat, kna, qct, kc, vct, knc = _proj(x1, lw, tab)
        oat = _attention(qat, ka, vat, kna)
        oct = _attention(qct, kc, vct, knc)
        x2 = _merge(x1, oat, oct, _dilated(qkvb, bias), lw)
        x = _ffn(x2.reshape(b * s, d), lw["g_ffn2"], *lw["ffn2"],
                 ple_args=(p[i].reshape(b * s, PLE_DIM), lw["g_ple"], lw["w_pg"], lw["w_ple"])).reshape(b, s, d)
    return x


def kernel(x_prompt, x_sample, p_prompt, p_sample, g_ffn1, w_ffn1_in, w_ffn1_out, g_mix, w_in, g_cq, g_ckv, w_uq,
           w_ukv, g_qa, g_ka, g_qb, g_kb, rel_bias, g_qc, g_kc, w_gate, b_gate, w_oa, w_ob, w_oc, w_out, g_ffn2,
           w_ffn2_in, w_ffn2_out, g_ple, w_pg, w_ple):
    W = dict(g_ffn1=g_ffn1, w_ffn1_in=w_ffn1_in, w_ffn1_out=w_ffn1_out, g_mix=g_mix, w_in=w_in,
             g_cq=g_cq, g_ckv=g_ckv, w_uq=w_uq, w_ukv=w_ukv, g_qa=g_qa, g_ka=g_ka,
             g_qb=g_qb, g_kb=g_kb, g_qc=g_qc, g_kc=g_kc,
             w_gate=w_gate, b_gate=b_gate, w_oa=w_oa, w_ob=w_ob, w_oc=w_oc, w_out=w_out,
             g_ffn2=g_ffn2, w_ffn2_in=w_ffn2_in, w_ffn2_out=w_ffn2_out,
             g_ple=g_ple, w_pg=w_pg, w_ple=w_ple)
    layers = [_layer_weights(W, i) for i in range(g_mix.shape[0])]
    bias = _band_bias(rel_bias)
    return (_trunk(x_prompt, p_prompt, layers, bias), _trunk(x_sample, p_sample, layers, bias))
```

```python
import functools
import math

import numpy as np
import jax
import jax.numpy as jnp
from jax import lax
from jax.experimental import pallas as pl
from jax.experimental.pallas import tpu as pltpu

F32 = jnp.float32
BF16 = jnp.bfloat16

D_MODEL = 1024
GRID_W = 64
PLE_DIM = 256
D_FF = 2816
NORM_EPS = 1e-6
ROPE_THETA = 10000.0
NEG_INF = -1e30

HA = 8
Q_RANK = 256
KV_RANK = 128
NOPE_A = 64
ROPE_A = 32
V_A = 64
QK_A = NOPE_A + ROPE_A

HD = 64
DIL_PAIRS = ((128, 1), (512, 4), (2048, 16))
N_GROUPS_B = 3
HPG_B = 4
N_HEADS_B = N_GROUPS_B * HPG_B
T5_BUCKETS = 32
T5_MAX_DIST = 1024
BAND_R = 64
DIL_TILE = BAND_R * max(d for _, d in DIL_PAIRS)

HC = 8
KVC = 2
GC = HC // KVC

A_COLS = Q_RANK + KV_RANK + ROPE_A
B_COLS = 3 * N_HEADS_B * HD
C_COLS = (HC + 2 * KVC) * HD
GROUP_B_COLS = HPG_B * HD
LANES = 128
LANE_HALVES = GROUP_B_COLS // LANES

ROPE_HALF = 16
QK_PAD = 128
LOG2E = math.log2(math.e)

VMEM_LIMIT_BYTES = 56 * 1024 * 1024
FFN_CHUNK = 256
SCORE_LOOKAHEAD = 2
DENOM_FLOOR = 2.0 ** -80


def _cparams(semantics):
    return pltpu.CompilerParams(dimension_semantics=semantics, vmem_limit_bytes=VMEM_LIMIT_BYTES)


def _const_spec(shape):
    nd = len(shape)
    return pl.BlockSpec(shape, lambda *_: (0,) * nd, pipeline_mode=pl.Buffered(1))


def _rms_rows(x, g):
    ms = jnp.mean(x * x, axis=-1, keepdims=True)
    return x * lax.rsqrt(ms + NORM_EPS) * g


def _rms_cols(x, g):
    ms = jnp.mean(x * x, axis=0, keepdims=True)
    return x * lax.rsqrt(ms + NORM_EPS) * g


def _rope_cols(x, cos, sin):
    x1, x2 = x[:ROPE_HALF], x[ROPE_HALF:]
    return x1 * cos - x2 * sin, x1 * sin + x2 * cos


def _ffn_kernel(*refs, n_chunks, ple):
    if ple:
        x_ref, g_ref, win_ref, wout_ref, pe_ref, gple_ref, wpg_ref, wple_ref, o_ref, acc_ref = refs
    else:
        x_ref, g_ref, win_ref, wout_ref, o_ref, acc_ref = refs
    x = x_ref[...]
    u = _rms_rows(x, g_ref[...]).astype(BF16)
    for c in range(n_chunks):
        a = jnp.dot(u, win_ref[c], preferred_element_type=F32)
        b = jnp.dot(u, win_ref[n_chunks + c], preferred_element_type=F32)
        h = (a * jax.nn.sigmoid(a) * b).astype(BF16)
        d = jnp.dot(h, wout_ref[c], preferred_element_type=F32)
        if c == 0:
            acc_ref[...] = d
        elif c < n_chunks - 1:
            acc_ref[...] += d
        else:
            y = x + 0.5 * (acc_ref[...] + d)
    if ple:
        gate = jax.nn.sigmoid(
            jnp.dot(_rms_rows(y, gple_ref[...]).astype(BF16), wpg_ref[...], preferred_element_type=F32))
        y = y + gate * jnp.dot(pe_ref[...].astype(BF16), wple_ref[...], preferred_element_type=F32)
    o_ref[...] = y


def _ffn(x2d, g, w_in_c, w_out_c, ple_args=None, tm=1024):
    n, d = x2d.shape
    n_chunks = w_out_c.shape[0]
    tm = min(tm, n)
    row = lambda i: (i, 0)
    in_specs = [pl.BlockSpec((tm, d), row), _const_spec(g.shape), _const_spec(w_in_c.shape), _const_spec(w_out_c.shape)]
    args = [x2d, g, w_in_c, w_out_c]
    if ple_args is not None:
        pe2d, g_ple, w_pg, w_ple = ple_args
        in_specs += [pl.BlockSpec((tm, pe2d.shape[1]), row), _const_spec(g_ple.shape), _const_spec(w_pg.shape),
                     _const_spec(w_ple.shape)]
        args += [pe2d, g_ple, w_pg, w_ple]
    return pl.pallas_call(
        functools.partial(_ffn_kernel, n_chunks=n_chunks, ple=ple_args is not None),
        grid=(n // tm,),
        in_specs=in_specs,
        out_specs=pl.BlockSpec((tm, d), row),
        out_shape=jax.ShapeDtypeStruct((n, d), F32),
        scratch_shapes=[pltpu.VMEM((tm, d), F32)],
        compiler_params=_cparams(("parallel",)),
        name="ffn_ple" if ple_args is not None else "ffn",
    )(*args)


_T_CQ = 0
_T_CKV = Q_RANK
_T_KR = Q_RANK + KV_RANK
_T_QC = A_COLS
_T_KC = A_COLS + HC * HD
_T_VC = A_COLS + (HC + KVC) * HD
_T_ROWS = A_COLS + C_COLS


def _proj_kernel(x_ref, gmix_ref, wb_ref, wact_ref, ones_ref, gqb_ref, gkb_ref, gcq_ref, gckv_ref, wuqt_ref,
                 wukvt_ref, gqa_ref, gka_ref, gqc_ref, gkc_ref, tab_ref,
                 qkvb_ref, qat_ref, ka_ref, vat_ref, kna_ref, qct_ref, kc_ref, vct_ref, knc_ref, *, scale_a, scale_c):
    ts = x_ref.shape[0]

    def key_tail(rows):
        r = lax.broadcasted_iota(jnp.int32, (rows, ts), 0)
        return jnp.where(r == rows - 1, 1.0, 0.0).astype(F32)

    def key_norm(kt):
        return jnp.sqrt(jnp.sum(kt * kt, axis=0, keepdims=True))

    ub = _rms_rows(x_ref[...], gmix_ref[...]).astype(BF16)

    pb = jnp.dot(ub, wb_ref[...], preferred_element_type=F32)
    ones = ones_ref[...]
    for part, g_ref in ((0, gqb_ref), (1, gkb_ref), (2, None)):
        for c in range(N_GROUPS_B):
            blk = part * N_GROUPS_B + c
            xx = pb[:, blk * GROUP_B_COLS:(blk + 1) * GROUP_B_COLS]
            if g_ref is not None:
                ms = jnp.dot((xx * xx).astype(BF16), ones, preferred_element_type=F32)
                xx = xx * lax.rsqrt(ms + NORM_EPS) * g_ref[...]
            for half in range(LANE_HALVES):
                qkvb_ref[LANE_HALVES * blk + half] = xx[:, half * LANES:(half + 1) * LANES]

    pt = lax.dot_general(wact_ref[...], ub, (((1,), (1,)), ((), ())), preferred_element_type=F32)
    cos_p, sin_p, cos_r, sin_r, cos_c, sin_c = (tab_ref[i] for i in range(6))

    cq = _rms_cols(pt[_T_CQ:_T_CQ + Q_RANK], gcq_ref[...]).astype(BF16)
    qa = jnp.dot(wuqt_ref[...], cq, preferred_element_type=F32)
    ckv = _rms_cols(pt[_T_CKV:_T_CKV + KV_RANK], gckv_ref[...]).astype(BF16)
    kv = jnp.dot(wukvt_ref[...], ckv, preferred_element_type=F32)
    gqa = gqa_ref[...]
    gka = gka_ref[...]
    kr1, kr2 = _rope_cols(_rms_cols(pt[_T_KR:_T_KR + ROPE_A], gka[NOPE_A:]), cos_p, sin_p)
    k_tail = key_tail(QK_PAD - QK_A)
    kr_sq = jnp.sum(kr1 * kr1 + kr2 * kr2, axis=0, keepdims=True)
    for h in range(HA):
        q0 = h * QK_A
        qn = _rms_cols(qa[q0:q0 + NOPE_A], gqa[:NOPE_A])
        q1, q2 = _rope_cols(_rms_cols(qa[q0 + NOPE_A:q0 + QK_A], gqa[NOPE_A:]), cos_p, sin_p)
        qat_ref[h, 0:NOPE_A] = (qn * scale_a).astype(BF16)
        qat_ref[h, NOPE_A:NOPE_A + ROPE_HALF] = (q1 * scale_a).astype(BF16)
        qat_ref[h, NOPE_A + ROPE_HALF:QK_A] = (q2 * scale_a).astype(BF16)
        qat_ref[h, QK_A:QK_PAD] = jnp.zeros((QK_PAD - QK_A, ts), BF16)
        k0 = h * (NOPE_A + V_A)
        kn = _rms_cols(kv[k0:k0 + NOPE_A], gka[:NOPE_A])
        kt = jnp.concatenate([kn, kr1, kr2, k_tail], axis=0)
        ka_ref[h] = kt.T.astype(BF16)
        kna_ref[h] = jnp.sqrt(jnp.sum(kn * kn, axis=0, keepdims=True) + kr_sq)
        vat_ref[h] = kv[k0 + NOPE_A:k0 + NOPE_A + V_A].astype(BF16)

    def axial(t):
        a1, a2 = _rope_cols(t[:HD // 2], cos_r, sin_r)
        b1, b2 = _rope_cols(t[HD // 2:], cos_c, sin_c)
        return a1, a2, b1, b2

    gqc = gqc_ref[...]
    gkc = gkc_ref[...]
    for h in range(HC):
        parts = axial(_rms_cols(pt[_T_QC + h * HD:_T_QC + (h + 1) * HD], gqc) * scale_c)
        qct_ref[h, 0:HD] = jnp.concatenate(parts, axis=0).astype(BF16)
        qct_ref[h, HD:QK_PAD] = jnp.zeros((QK_PAD - HD, ts), BF16)
    c_tail = key_tail(QK_PAD - HD)
    for h in range(KVC):
        kh = jnp.concatenate(axial(_rms_cols(pt[_T_KC + h * HD:_T_KC + (h + 1) * HD], gkc)), axis=0)
        kt = jnp.concatenate([kh, c_tail], axis=0)
        kc_ref[h] = kt.T.astype(BF16)
        knc_ref[h] = key_norm(kh)
        vct_ref[h] = pt[_T_VC + h * HD:_T_VC + (h + 1) * HD].astype(BF16)


def _proj(x3d, lw, tab, ts=512):
    b, s, d = x3d.shape
    ts = min(ts, s)
    consts = [lw["g_mix"], lw["w_b"], lw["w_act"], lw["ones_blk"], lw["g_qb"], lw["g_kb"], lw["g_cq"], lw["g_ckv"],
              lw["w_uqt"], lw["w_ukvt"], lw["g_qa"], lw["g_ka"], lw["g_qc"], lw["g_kc"]]
    in_specs = ([pl.BlockSpec((None, ts, d), lambda bi, i: (bi, i, 0))] + [_const_spec(c.shape) for c in consts]
                + [pl.BlockSpec((6, ROPE_HALF, ts), lambda bi, i: (0, 0, i))])
    tok_major = lambda h: pl.BlockSpec((None, h, ts, QK_PAD), lambda bi, i: (bi, 0, i, 0))
    feat_major = lambda h, r: pl.BlockSpec((None, h, r, ts), lambda bi, i: (bi, 0, 0, i))
    out_specs = [pl.BlockSpec((None, B_COLS // LANES, ts, LANES), lambda bi, i: (bi, 0, i, 0)),
                 feat_major(HA, QK_PAD), tok_major(HA), feat_major(HA, V_A), feat_major(HA, 1),
                 feat_major(HC, QK_PAD), tok_major(KVC), feat_major(KVC, HD), feat_major(KVC, 1)]
    out_shape = [jax.ShapeDtypeStruct((b, B_COLS // LANES, s, LANES), F32),
                 jax.ShapeDtypeStruct((b, HA, QK_PAD, s), BF16), jax.ShapeDtypeStruct((b, HA, s, QK_PAD), BF16),
                 jax.ShapeDtypeStruct((b, HA, V_A, s), BF16), jax.ShapeDtypeStruct((b, HA, 1, s), F32),
                 jax.ShapeDtypeStruct((b, HC, QK_PAD, s), BF16), jax.ShapeDtypeStruct((b, KVC, s, QK_PAD), BF16),
                 jax.ShapeDtypeStruct((b, KVC, HD, s), BF16), jax.ShapeDtypeStruct((b, KVC, 1, s), F32)]
    return pl.pallas_call(
        functools.partial(_proj_kernel, scale_a=QK_A ** -0.5 * LOG2E, scale_c=HD ** -0.5 * LOG2E),
        grid=(b, s // ts),
        in_specs=in_specs,
        out_specs=out_specs,
        out_shape=out_shape,
        compiler_params=_cparams(("parallel", "parallel")),
        name="proj",
    )(x3d, *consts, tab)


def _attn_kernel(kmax_ref, qt_ref, k_ref, vt_ref, ot_ref, acc_ref, den_ref, m_ref, *, tk, n_sub):
    s_len = k_ref.shape[0]
    qt = qt_ref[...]
    qf = qt.astype(F32)
    qn = jnp.sqrt(jnp.sum(qf * qf, axis=0, keepdims=True))
    bound = qn * kmax_ref[0:1, 0:1] * (1.0 + 2.0 ** -6) + 1.0
    row = lax.broadcasted_iota(jnp.int32, qt.shape, 0)
    qa = jnp.where(row == QK_PAD - 1, -bound, qf).astype(BF16)

    acc_ref[...] = jnp.zeros_like(acc_ref)
    den_ref[...] = jnp.zeros_like(den_ref)
    span = tk * n_sub

    def bounded(jj, carry):
        off = jj * span
        chunk = lambda u: pl.ds(pl.multiple_of(off + u * tk, tk), tk)
        score = lambda u: jnp.dot(k_ref[chunk(u), :], qa, preferred_element_type=F32)
        st = [score(u) for u in range(min(SCORE_LOOKAHEAD, n_sub))]
        tot = den = None
        for u in range(n_sub):
            if u + SCORE_LOOKAHEAD < n_sub:
                st.append(score(u + SCORE_LOOKAHEAD))
            p = jnp.exp2(st[u])
            pv = jnp.dot(vt_ref[:, chunk(u)], p.astype(BF16), preferred_element_type=F32)
            ps = jnp.sum(p, axis=0, keepdims=True)
            tot = pv if tot is None else tot + pv
            den = ps if den is None else den + ps
        acc_ref[...] += tot
        den_ref[...] += den
        return carry

    lax.fori_loop(0, s_len // span, bounded, 0)
    trusted = jnp.min(den_ref[...]) >= DENOM_FLOOR

    @pl.when(jnp.logical_not(trusted))
    def _():
        m_ref[...] = jnp.full_like(m_ref, NEG_INF)
        acc_ref[...] = jnp.zeros_like(acc_ref)
        den_ref[...] = jnp.zeros_like(den_ref)

        def running_max(j, carry):
            o = pl.multiple_of(j * tk, tk)
            st = jnp.dot(k_ref[pl.ds(o, tk), :], qt, preferred_element_type=F32)
            m_old = m_ref[...]
            m_new = jnp.maximum(m_old, jnp.max(st, axis=0, keepdims=True))
            alpha = jnp.exp2(m_old - m_new)
            p = jnp.exp2(st - m_new)
            acc_ref[...] = alpha * acc_ref[...] + jnp.dot(
                vt_ref[:, pl.ds(o, tk)], p.astype(BF16), preferred_element_type=F32)
            den_ref[...] = alpha * den_ref[...] + jnp.sum(p, axis=0, keepdims=True)
            m_ref[...] = m_new
            return carry

        lax.fori_loop(0, s_len // tk, running_max, 0)

    ot_ref[...] = (acc_ref[...] / den_ref[...]).astype(ot_ref.dtype)


def _attention(qt, k, vt, knorm, tq=512, tk=256, n_sub=64):
    b, hq, _, s = qt.shape
    hk = k.shape[1]
    rep = hq // hk
    tq = min(tq, s)
    tk = min(tk, s)
    n_sub = min(n_sub, s // tk)
    kmax = jnp.broadcast_to(jnp.max(knorm, axis=(2, 3))[:, :, None, None], (b, hk, 8, 128))
    return pl.pallas_call(
        functools.partial(_attn_kernel, tk=tk, n_sub=n_sub),
        grid=(b, hq, s // tq),
        in_specs=[pl.BlockSpec((None, None, 8, 128), lambda bi, h, i: (bi, h // rep, 0, 0)),
                  pl.BlockSpec((None, None, QK_PAD, tq), lambda bi, h, i: (bi, h, 0, i)),
                  pl.BlockSpec((None, None, s, QK_PAD), lambda bi, h, i: (bi, h // rep, 0, 0)),
                  pl.BlockSpec((None, None, V_A, s), lambda bi, h, i: (bi, h // rep, 0, 0))],
        out_specs=pl.BlockSpec((None, None, V_A, tq), lambda bi, h, i: (bi, h, 0, i)),
        out_shape=jax.ShapeDtypeStruct((b, hq, V_A, s), BF16),
        scratch_shapes=[pltpu.VMEM((V_A, tq), F32), pltpu.VMEM((1, tq), F32), pltpu.VMEM((1, tq), F32)],
        compiler_params=_cparams(("parallel", "parallel", "arbitrary")),
        name="attention",
    )(kmax, qt, k, vt)


def _dilated_kernel(*refs, n_t):
    n_in = 7 * N_GROUPS_B
    bias_ref, o_ref, kbuf, vbuf, os_ref, ls_ref = refs[n_in:]
    tile = o_ref.shape[0]
    rad = BAND_R
    ti = pl.program_id(1)
    col = lax.broadcasted_iota(jnp.int32, (1, 3 * rad), 1)
    edge_lo = jnp.where((col < rad) & (ti == 0), NEG_INF, 0.0)
    edge_hi = jnp.where((col >= 2 * rad) & (ti == n_t - 1), NEG_INF, 0.0)
    head_of_lane = lax.broadcasted_iota(jnp.int32, (1, GROUP_B_COLS), 1) // HD
    for g, (_, dil) in enumerate(DIL_PAIRS):
        q_ref, kp_ref, kc_ref, kn_ref, vp_ref, vc_ref, vn_ref = refs[7 * g:7 * g + 7]
        lblk = tile // dil
        n_blk = lblk // rad

        def class_rows(r, start, size):
            return pl.ds(start, size) if dil == 1 else pl.ds(r + dil * start, size, stride=dil)

        def rows(ref, r, start, size):
            sl = class_rows(r, start, size)
            return jnp.concatenate([ref[half, sl, :] for half in range(LANE_HALVES)], axis=1)

        for r in range(dil):
            for buf, p_ref, c_ref, n_ref in ((kbuf, kp_ref, kc_ref, kn_ref), (vbuf, vp_ref, vc_ref, vn_ref)):
                buf[0:rad] = rows(p_ref, r, 0, rad).astype(BF16)
                buf[rad:rad + lblk] = rows(c_ref, r, 0, lblk).astype(BF16)
                buf[rad + lblk:2 * rad + lblk] = rows(n_ref, r, 0, rad).astype(BF16)
            for c in range(n_blk):
                q = (rows(q_ref, r, c * rad, rad) * (HD ** -0.5)).astype(BF16)
                kw = kbuf[c * rad:(c + 3) * rad, :]
                vw = vbuf[c * rad:(c + 3) * rad, :]
                qs = jnp.concatenate([jnp.where(head_of_lane == h, q, jnp.zeros_like(q)) for h in range(HPG_B)],
                                     axis=0)
                s = lax.dot_general(qs, kw, (((1,), (1,)), ((), ())), preferred_element_type=F32)
                s = s + bias_ref[g]
                if c == 0:
                    s = s + edge_lo
                if c == n_blk - 1:
                    s = s + edge_hi
                m = jnp.max(s, axis=-1, keepdims=True)
                p = jnp.exp(s - m)
                l = jnp.sum(p, axis=-1, keepdims=True)
                ov = jnp.dot((p / l).astype(BF16), vw, preferred_element_type=F32)
                lse = m + jnp.log(l)
                o_acc = ov[(HPG_B - 1) * rad:]
                lse_acc = jnp.broadcast_to(lse[(HPG_B - 1) * rad:], (rad, GROUP_B_COLS))
                for h in range(HPG_B - 1):
                    sel = head_of_lane == h
                    o_acc = jnp.where(sel, ov[h * rad:(h + 1) * rad], o_acc)
                    lse_acc = jnp.where(sel, lse[h * rad:(h + 1) * rad], lse_acc)
                dst = class_rows(r, c * rad, rad)
                for half in range(LANE_HALVES):
                    os_ref[g, half, dst, :] = o_acc[:, half * LANES:(half + 1) * LANES]
                    ls_ref[g, half, dst, :] = lse_acc[:, half * LANES:(half + 1) * LANES]

    for half in range(LANE_HALVES):
        l0, l1, l2 = ls_ref[0, half], ls_ref[1, half], ls_ref[2, half]
        m = jnp.maximum(jnp.maximum(l0, l1), l2)
        e0, e1, e2 = jnp.exp(l0 - m), jnp.exp(l1 - m), jnp.exp(l2 - m)
        inv = 1.0 / (e0 + e1 + e2)
        o_ref[:, half * LANES:(half + 1) * LANES] = (
            (e0 * inv) * os_ref[0, half] + (e1 * inv) * os_ref[1, half] + (e2 * inv) * os_ref[2, half]
        ).astype(o_ref.dtype)


def _dilated(qkvb, bias):
    b, _, s, _ = qkvb.shape
    tile = DIL_TILE
    n_t = s // tile
    in_specs = []
    for g, (_, dil) in enumerate(DIL_PAIRS):
        halo = BAND_R * dil
        per_tile = tile // halo
        n_halo = s // halo

        def cur(part, g=g):
            return pl.BlockSpec((None, LANE_HALVES, tile, LANES), lambda bi, i: (bi, part * N_GROUPS_B + g, i, 0))

        def prev(part, g=g, halo=halo, per_tile=per_tile):
            return pl.BlockSpec((None, LANE_HALVES, halo, LANES),
                                lambda bi, i: (bi, part * N_GROUPS_B + g, jnp.maximum(i * per_tile - 1, 0), 0))

        def nxt(part, g=g, halo=halo, per_tile=per_tile, n_halo=n_halo):
            return pl.BlockSpec((None, LANE_HALVES, halo, LANES),
                                lambda bi, i: (bi, part * N_GROUPS_B + g, jnp.minimum((i + 1) * per_tile, n_halo - 1), 0))

        in_specs += [cur(0), prev(1), cur(1), nxt(1), prev(2), cur(2), nxt(2)]
    in_specs.append(pl.BlockSpec(bias.shape, lambda bi, i: (0, 0, 0)))
    return pl.pallas_call(
        functools.partial(_dilated_kernel, n_t=n_t),
        grid=(b, n_t),
        in_specs=in_specs,
        out_specs=pl.BlockSpec((None, tile, GROUP_B_COLS), lambda bi, i: (bi, i, 0)),
        out_shape=jax.ShapeDtypeStruct((b, s, GROUP_B_COLS), BF16),
        scratch_shapes=[pltpu.VMEM((tile + 2 * BAND_R, GROUP_B_COLS), BF16),
                        pltpu.VMEM((tile + 2 * BAND_R, GROUP_B_COLS), BF16),
                        pltpu.VMEM((N_GROUPS_B, LANE_HALVES, tile, LANES), F32),
                        pltpu.VMEM((N_GROUPS_B, LANE_HALVES, tile, LANES), F32)],
        compiler_params=_cparams(("parallel", "parallel")),
        name="dilated",
    )(*([qkvb] * (7 * N_GROUPS_B)), bias)


def _merge_kernel(x_ref, gmix_ref, oat_ref, oct_ref, ob_ref, wg_ref, bg_ref, woa_ref, wob_ref, woc_ref, wout_ref,
                  o_ref):
    x = x_ref[...]
    ub = _rms_rows(x, gmix_ref[...]).astype(BF16)
    d = x.shape[1]
    tn = (((0,), (0,)), ((), ()))
    ya = lax.dot_general(oat_ref[...], woa_ref[...], tn, preferred_element_type=F32)
    yb = jnp.dot(ob_ref[...], wob_ref[...], preferred_element_type=F32)
    yc = lax.dot_general(oct_ref[...], woc_ref[...], tn, preferred_element_type=F32)

    def gate(n):
        z = jnp.dot(ub, wg_ref[:, n * d:(n + 1) * d], preferred_element_type=F32) + bg_ref[:, n * d:(n + 1) * d]
        return jax.nn.sigmoid(z)

    merged = gate(0) * ya + gate(1) * yb + gate(2) * yc
    o_ref[...] = x + jnp.dot(merged.astype(BF16), wout_ref[...], preferred_element_type=F32)


def _merge(x3d, oat, oct, ob, lw, ts=512):
    b, s, d = x3d.shape
    ts = min(ts, s)
    tok = lambda w: pl.BlockSpec((None, ts, w), lambda bi, i: (bi, i, 0))
    feat = lambda r: pl.BlockSpec((None, r, ts), lambda bi, i: (bi, 0, i))
    consts = [lw["w_gate"], lw["b_gate"], lw["w_oa"], lw["w_ob"], lw["w_oc"], lw["w_out"]]
    return pl.pallas_call(
        _merge_kernel,
        grid=(b, s // ts),
        in_specs=[tok(d), _const_spec(lw["g_mix"].shape), feat(HA * V_A), feat(HC * HD), tok(GROUP_B_COLS)]
                 + [_const_spec(c.shape) for c in consts],
        out_specs=tok(d),
        out_shape=jax.ShapeDtypeStruct((b, s, d), F32),
        compiler_params=_cparams(("parallel", "parallel")),
        name="merge",
    )(x3d, lw["g_mix"], oat.reshape(b, HA * V_A, s), oct.reshape(b, HC * HD, s), ob, *consts)


def _t5_bucket(rel):
    nb = T5_BUCKETS // 2
    max_exact = nb // 2
    n = np.abs(rel)
    large = max_exact + (np.log(np.maximum(n, 1) / max_exact) / math.log(T5_MAX_DIST / max_exact)
                         * (nb - max_exact)).astype(np.int32)
    large = np.minimum(large, nb - 1)
    return (rel > 0).astype(np.int32) * nb + np.where(n < max_exact, n, large).astype(np.int32)


def _band_bias(rel_bias):
    qi = np.arange(BAND_R)[:, None]
    kj = np.arange(3 * BAND_R)[None, :]
    rel = kj - BAND_R - qi
    band = np.where(np.abs(rel) <= BAND_R, 0.0, NEG_INF).astype(np.float32)
    out = []
    for g, (_, dil) in enumerate(DIL_PAIRS):
        tab = rel_bias[:, g * HPG_B:(g + 1) * HPG_B]
        bias = jnp.transpose(tab[_t5_bucket(rel * dil)], (2, 0, 1)).astype(F32)
        out.append((bias + band[None]).reshape(HPG_B * BAND_R, 3 * BAND_R))
    return jnp.stack(out, axis=0)


def _rope_tables(s):
    freqs = (ROPE_THETA ** (-np.arange(ROPE_HALF) / ROPE_HALF)).astype(np.float32)
    pos = jnp.arange(s, dtype=jnp.int32)
    rows = jnp.repeat(jnp.arange(s // GRID_W, dtype=jnp.int32), GRID_W)
    cols = pos % GRID_W
    tabs = []
    for p in (pos, rows, cols):
        ang = p.astype(F32)[:, None] * jnp.asarray(freqs)[None, :]
        tabs += [jnp.cos(ang).T, jnp.sin(ang).T]
    return jnp.stack(tabs, axis=0)


def _ffn_weights(w_in, w_out):
    d = w_in.shape[0]
    n_chunks = D_FF // FFN_CHUNK
    w_in_c = w_in.astype(BF16).reshape(d, 2 * n_chunks, FFN_CHUNK).transpose(1, 0, 2)
    w_out_c = w_out.astype(BF16).reshape(n_chunks, FFN_CHUNK, d)
    return w_in_c, w_out_c


def _layer_weights(W, i):
    row = lambda v: v.reshape(1, -1).astype(F32)
    colv = lambda v: v.reshape(-1, 1).astype(F32)
    w_in = W["w_in"][i]
    w_a = w_in[:, :A_COLS]
    w_b = w_in[:, A_COLS:A_COLS + B_COLS]
    w_c = w_in[:, A_COLS + B_COLS:]
    head = np.arange(GROUP_B_COLS) // HD
    ones_blk = jnp.asarray((head[:, None] == head[None, :]).astype(np.float32) / HD, dtype=BF16)
    lw = dict(
        g_ffn1=row(W["g_ffn1"][i]), g_ffn2=row(W["g_ffn2"][i]), g_mix=row(W["g_mix"][i]), g_ple=row(W["g_ple"][i]),
        w_b=w_b.astype(BF16),
        w_act=jnp.concatenate([w_a, w_c], axis=1).T.astype(BF16),
        ones_blk=ones_blk,
        g_qb=row(jnp.tile(W["g_qb"][i], HPG_B)), g_kb=row(jnp.tile(W["g_kb"][i], HPG_B)),
        g_cq=colv(W["g_cq"][i]), g_ckv=colv(W["g_ckv"][i]),
        w_uqt=W["w_uq"][i].T.astype(BF16), w_ukvt=W["w_ukv"][i].T.astype(BF16),
        g_qa=colv(W["g_qa"][i]), g_ka=colv(W["g_ka"][i]), g_qc=colv(W["g_qc"][i]), g_kc=colv(W["g_kc"][i]),
        w_gate=W["w_gate"][i].astype(BF16), b_gate=row(W["b_gate"][i]),
        w_oa=W["w_oa"][i].astype(BF16), w_ob=W["w_ob"][i].astype(BF16), w_oc=W["w_oc"][i].astype(BF16),
        w_out=W["w_out"][i].astype(BF16),
        w_pg=W["w_pg"][i].astype(BF16), w_ple=W["w_ple"][i].astype(BF16),
    )
    lw["ffn1"] = _ffn_weights(W["w_ffn1_in"][i], W["w_ffn1_out"][i])
    lw["ffn2"] = _ffn_weights(W["w_ffn2_in"][i], W["w_ffn2_out"][i])
    return lw


def _trunk(x, p, layers, bias):
    b, s, d = x.shape
    tab = _rope_tables(s)
    for i, lw in enumerate(layers):
        x1 = _ffn(x.reshape(b * s, d), lw["g_ffn1"], *lw["ffn1"]).reshape(b, s, d)
        qkvb, qat, ka, vat, kna, qct, kc, vct, knc = _proj(x1, lw, tab)
        oat = _attention(qat, ka, vat, kna)
        oct = _attention(qct, kc, vct, knc)
        x2 = _merge(x1, oat, oct, _dilated(qkvb, bias), lw)
        x = _ffn(x2.reshape(b * s, d), lw["g_ffn2"], *lw["ffn2"],
                 ple_args=(p[i].reshape(b * s, PLE_DIM), lw["g_ple"], lw["w_pg"], lw["w_ple"])).reshape(b, s, d)
    return x


def kernel(x_prompt, x_sample, p_prompt, p_sample, g_ffn1, w_ffn1_in, w_ffn1_out, g_mix, w_in, g_cq, g_ckv, w_uq,
           w_ukv, g_qa, g_ka, g_qb, g_kb, rel_bias, g_qc, g_kc, w_gate, b_gate, w_oa, w_ob, w_oc, w_out, g_ffn2,
           w_ffn2_in, w_ffn2_out, g_ple, w_pg, w_ple):
    W = dict(g_ffn1=g_ffn1, w_ffn1_in=w_ffn1_in, w_ffn1_out=w_ffn1_out, g_mix=g_mix, w_in=w_in,
             g_cq=g_cq, g_ckv=g_ckv, w_uq=w_uq, w_ukv=w_ukv, g_qa=g_qa, g_ka=g_ka,
             g_qb=g_qb, g_kb=g_kb, g_qc=g_qc, g_kc=g_kc,
             w_gate=w_gate, b_gate=b_gate, w_oa=w_oa, w_ob=w_ob, w_oc=w_oc, w_out=w_out,
             g_ffn2=g_ffn2, w_ffn2_in=w_ffn2_in, w_ffn2_out=w_ffn2_out,
             g_ple=g_ple, w_pg=w_pg, w_ple=w_ple)
    layers = [_layer_weights(W, i) for i in range(g_mix.shape[0])]
    bias = _band_bias(rel_bias)
    return (_trunk(x_prompt, p_prompt, layers, bias), _trunk(x_sample, p_sample, layers, bias))
```

```python
import functools
import math

import numpy as np
import jax
import jax.numpy as jnp
from jax import lax
from jax.experimental import pallas as pl
from jax.experimental.pallas import tpu as pltpu

F32 = jnp.float32
BF16 = jnp.bfloat16
FP8 = jnp.float8_e4m3fn

D_MODEL = 1024
GRID_W = 64
PLE_DIM = 256
D_FF = 2816
NORM_EPS = 1e-6
ROPE_THETA = 10000.0
NEG_INF = -1e30

HA = 8
Q_RANK = 256
KV_RANK = 128
NOPE_A = 64
ROPE_A = 32
V_A = 64
QK_A = NOPE_A + ROPE_A

HD = 64
DIL_PAIRS = ((128, 1), (512, 4), (2048, 16))
N_GROUPS_B = 3
HPG_B = 4
N_HEADS_B = N_GROUPS_B * HPG_B
T5_BUCKETS = 32
T5_MAX_DIST = 1024
BAND_R = 64
DIL_TILE = BAND_R * max(d for _, d in DIL_PAIRS)

HC = 8
KVC = 2
GC = HC // KVC

A_COLS = Q_RANK + KV_RANK + ROPE_A
B_COLS = 3 * N_HEADS_B * HD
C_COLS = (HC + 2 * KVC) * HD
GROUP_B_COLS = HPG_B * HD
LANES = 128
LANE_HALVES = GROUP_B_COLS // LANES

ROPE_HALF = 16
QK_PAD = 128
K8_COLS = 4 * HD
Q8_SCALE = 4.0
LOG2E = math.log2(math.e)

VMEM_LIMIT_BYTES = 56 * 1024 * 1024
FFN_CHUNK = 256
SCORE_LOOKAHEAD = 2
SCORE_LOOKAHEAD_FP8 = 3
DENOM_FLOOR = 2.0 ** -80
DENOM_CEIL = 2.0 ** 60
FP8_MAX = 448.0


def _cparams(semantics):
    return pltpu.CompilerParams(dimension_semantics=semantics, vmem_limit_bytes=VMEM_LIMIT_BYTES)


def _const_spec(shape):
    nd = len(shape)
    return pl.BlockSpec(shape, lambda *_: (0,) * nd, pipeline_mode=pl.Buffered(1))


def _rms_rows(x, g):
    ms = jnp.mean(x * x, axis=-1, keepdims=True)
    return x * lax.rsqrt(ms + NORM_EPS) * g


def _rms_cols(x, g):
    ms = jnp.mean(x * x, axis=0, keepdims=True)
    return x * lax.rsqrt(ms + NORM_EPS) * g


def _rope_cols(x, cos, sin):
    x1, x2 = x[:ROPE_HALF], x[ROPE_HALF:]
    return x1 * cos - x2 * sin, x1 * sin + x2 * cos


def _ffn_kernel(*refs, n_chunks, ple):
    if ple:
        x_ref, g_ref, win_ref, wout_ref, pe_ref, gple_ref, wpg_ref, wple_ref, o_ref, acc_ref = refs
    else:
        x_ref, g_ref, win_ref, wout_ref, o_ref, acc_ref = refs
    x = x_ref[...]
    u = _rms_rows(x, g_ref[...]).astype(BF16)
    for c in range(n_chunks):
        a = jnp.dot(u, win_ref[c], preferred_element_type=F32)
        b = jnp.dot(u, win_ref[n_chunks + c], preferred_element_type=F32)
        h = (a * jax.nn.sigmoid(a) * b).astype(BF16)
        d = jnp.dot(h, wout_ref[c], preferred_element_type=F32)
        if c == 0:
            acc_ref[...] = d
        elif c < n_chunks - 1:
            acc_ref[...] += d
        else:
            y = x + 0.5 * (acc_ref[...] + d)
    if ple:
        gate = jax.nn.sigmoid(
            jnp.dot(_rms_rows(y, gple_ref[...]).astype(BF16), wpg_ref[...], preferred_element_type=F32))
        y = y + gate * jnp.dot(pe_ref[...].astype(BF16), wple_ref[...], preferred_element_type=F32)
    o_ref[...] = y


def _ffn(x2d, g, w_in_c, w_out_c, ple_args=None, tm=1024):
    n, d = x2d.shape
    n_chunks = w_out_c.shape[0]
    tm = min(tm, n)
    row = lambda i: (i, 0)
    in_specs = [pl.BlockSpec((tm, d), row), _const_spec(g.shape), _const_spec(w_in_c.shape), _const_spec(w_out_c.shape)]
    args = [x2d, g, w_in_c, w_out_c]
    if ple_args is not None:
        pe2d, g_ple, w_pg, w_ple = ple_args
        in_specs += [pl.BlockSpec((tm, pe2d.shape[1]), row), _const_spec(g_ple.shape), _const_spec(w_pg.shape),
                     _const_spec(w_ple.shape)]
        args += [pe2d, g_ple, w_pg, w_ple]
    return pl.pallas_call(
        functools.partial(_ffn_kernel, n_chunks=n_chunks, ple=ple_args is not None),
        grid=(n // tm,),
        in_specs=in_specs,
        out_specs=pl.BlockSpec((tm, d), row),
        out_shape=jax.ShapeDtypeStruct((n, d), F32),
        scratch_shapes=[pltpu.VMEM((tm, d), F32)],
        compiler_params=_cparams(("parallel",)),
        name="ffn_ple" if ple_args is not None else "ffn",
    )(*args)


_T_CQ = 0
_T_CKV = Q_RANK
_T_KR = Q_RANK + KV_RANK
_T_QC = A_COLS
_T_KC = A_COLS + HC * HD
_T_VC = A_COLS + (HC + KVC) * HD
_T_ROWS = A_COLS + C_COLS


def _proj_kernel(x_ref, gmix_ref, wb_ref, wact_ref, ones_ref, gqb_ref, gkb_ref, gcq_ref, gckv_ref, wuqt_ref,
                 wukvt_ref, gqa_ref, gka_ref, gqc_ref, gkc_ref, tab_ref,
                 qkvb_ref, qat_ref, ka_ref, vat_ref, kna_ref, qct_ref, kc_ref, vct_ref, knc_ref, *, scale_a, scale_c):
    ts = x_ref.shape[0]

    def key_tail(rows):
        r = lax.broadcasted_iota(jnp.int32, (rows, ts), 0)
        return jnp.where(r == rows - 1, 1.0, 0.0).astype(F32)

    def key_tail_first(rows):
        r = lax.broadcasted_iota(jnp.int32, (rows, ts), 0)
        return jnp.where(r == 0, 1.0, 0.0).astype(F32)

    def key_norm(kt):
        return jnp.sqrt(jnp.sum(kt * kt, axis=0, keepdims=True))

    ub = _rms_rows(x_ref[...], gmix_ref[...]).astype(BF16)

    pb = jnp.dot(ub, wb_ref[...], preferred_element_type=F32)
    ones = ones_ref[...]
    for part, g_ref in ((0, gqb_ref), (1, gkb_ref), (2, None)):
        for c in range(N_GROUPS_B):
            blk = part * N_GROUPS_B + c
            xx = pb[:, blk * GROUP_B_COLS:(blk + 1) * GROUP_B_COLS]
            if g_ref is not None:
                ms = jnp.dot((xx * xx).astype(BF16), ones, preferred_element_type=F32)
                xx = xx * lax.rsqrt(ms + NORM_EPS) * g_ref[...]
            for half in range(LANE_HALVES):
                qkvb_ref[LANE_HALVES * blk + half] = xx[:, half * LANES:(half + 1) * LANES]

    pt = lax.dot_general(wact_ref[...], ub, (((1,), (1,)), ((), ())), preferred_element_type=F32)
    cos_p, sin_p, cos_r, sin_r, cos_c, sin_c = (tab_ref[i] for i in range(6))

    cq = _rms_cols(pt[_T_CQ:_T_CQ + Q_RANK], gcq_ref[...]).astype(BF16)
    qa = jnp.dot(wuqt_ref[...], cq, preferred_element_type=F32)
    ckv = _rms_cols(pt[_T_CKV:_T_CKV + KV_RANK], gckv_ref[...]).astype(BF16)
    kv = jnp.dot(wukvt_ref[...], ckv, preferred_element_type=F32)
    gqa = gqa_ref[...]
    gka = gka_ref[...]
    kr1, kr2 = _rope_cols(_rms_cols(pt[_T_KR:_T_KR + ROPE_A], gka[NOPE_A:]), cos_p, sin_p)
    k_tail = key_tail(QK_PAD - QK_A)
    kr_sq = jnp.sum(kr1 * kr1 + kr2 * kr2, axis=0, keepdims=True)
    for h in range(HA):
        q0 = h * QK_A
        qn = _rms_cols(qa[q0:q0 + NOPE_A], gqa[:NOPE_A])
        q1, q2 = _rope_cols(_rms_cols(qa[q0 + NOPE_A:q0 + QK_A], gqa[NOPE_A:]), cos_p, sin_p)
        qat_ref[h, 0:NOPE_A] = (qn * scale_a).astype(BF16)
        qat_ref[h, NOPE_A:NOPE_A + ROPE_HALF] = (q1 * scale_a).astype(BF16)
        qat_ref[h, NOPE_A + ROPE_HALF:QK_A] = (q2 * scale_a).astype(BF16)
        qat_ref[h, QK_A:QK_PAD] = jnp.zeros((QK_PAD - QK_A, ts), BF16)
        k0 = h * (NOPE_A + V_A)
        kn = _rms_cols(kv[k0:k0 + NOPE_A], gka[:NOPE_A])
        kt = jnp.concatenate([kn, kr1, kr2, k_tail], axis=0)
        ka_ref[h] = kt.T.astype(BF16)
        kna_ref[h] = jnp.sqrt(jnp.sum(kn * kn, axis=0, keepdims=True) + kr_sq)
        vat_ref[h] = kv[k0 + NOPE_A:k0 + NOPE_A + V_A].astype(BF16)

    def axial(t):
        a1, a2 = _rope_cols(t[:HD // 2], cos_r, sin_r)
        b1, b2 = _rope_cols(t[HD // 2:], cos_c, sin_c)
        return a1, a2, b1, b2

    gqc = gqc_ref[...]
    gkc = gkc_ref[...]
    for h in range(HC):
        parts = axial(_rms_cols(pt[_T_QC + h * HD:_T_QC + (h + 1) * HD], gqc) * scale_c)
        qct_ref[h, 0:HD] = jnp.concatenate(parts, axis=0).astype(BF16)
        qct_ref[h, HD:QK_PAD] = jnp.zeros((QK_PAD - HD, ts), BF16)
    c_tail = key_tail_first(HD)
    for h in range(KVC):
        kh = jnp.concatenate(axial(_rms_cols(pt[_T_KC + h * HD:_T_KC + (h + 1) * HD], gkc)), axis=0)
        ks = kh * (1.0 / Q8_SCALE)
        k_hi = ks.astype(FP8).astype(F32)
        k_lo = ks - k_hi
        for half, parts in enumerate(((k_hi, k_lo), (k_hi, c_tail))):
            kc_ref[h, :, half * LANES:(half + 1) * LANES] = jnp.concatenate(parts, axis=0).T.astype(FP8)
        knc_ref[h] = key_norm(kh)
        vct_ref[h] = pt[_T_VC + h * HD:_T_VC + (h + 1) * HD].astype(BF16)


def _proj(x3d, lw, tab, ts=512):
    b, s, d = x3d.shape
    ts = min(ts, s)
    consts = [lw["g_mix"], lw["w_b"], lw["w_act"], lw["ones_blk"], lw["g_qb"], lw["g_kb"], lw["g_cq"], lw["g_ckv"],
              lw["w_uqt"], lw["w_ukvt"], lw["g_qa"], lw["g_ka"], lw["g_qc"], lw["g_kc"]]
    in_specs = ([pl.BlockSpec((None, ts, d), lambda bi, i: (bi, i, 0))] + [_const_spec(c.shape) for c in consts]
                + [pl.BlockSpec((6, ROPE_HALF, ts), lambda bi, i: (0, 0, i))])
    tok_major = lambda h, w=QK_PAD: pl.BlockSpec((None, h, ts, w), lambda bi, i: (bi, 0, i, 0))
    feat_major = lambda h, r: pl.BlockSpec((None, h, r, ts), lambda bi, i: (bi, 0, 0, i))
    out_specs = [pl.BlockSpec((None, B_COLS // LANES, ts, LANES), lambda bi, i: (bi, 0, i, 0)),
                 feat_major(HA, QK_PAD), tok_major(HA), feat_major(HA, V_A), feat_major(HA, 1),
                 feat_major(HC, QK_PAD), tok_major(KVC, K8_COLS), feat_major(KVC, HD), feat_major(KVC, 1)]
    out_shape = [jax.ShapeDtypeStruct((b, B_COLS // LANES, s, LANES), F32),
                 jax.ShapeDtypeStruct((b, HA, QK_PAD, s), BF16), jax.ShapeDtypeStruct((b, HA, s, QK_PAD), BF16),
                 jax.ShapeDtypeStruct((b, HA, V_A, s), BF16), jax.ShapeDtypeStruct((b, HA, 1, s), F32),
                 jax.ShapeDtypeStruct((b, HC, QK_PAD, s), BF16), jax.ShapeDtypeStruct((b, KVC, s, K8_COLS), FP8),
                 jax.ShapeDtypeStruct((b, KVC, HD, s), BF16), jax.ShapeDtypeStruct((b, KVC, 1, s), F32)]
    return pl.pallas_call(
        functools.partial(_proj_kernel, scale_a=QK_A ** -0.5 * LOG2E, scale_c=HD ** -0.5 * LOG2E),
        grid=(b, s // ts),
        in_specs=in_specs,
        out_specs=out_specs,
        out_shape=out_shape,
        compiler_params=_cparams(("parallel", "parallel")),
        name="proj",
    )(x3d, *consts, tab)


def _attn_kernel(kmax_ref, qt_ref, k_ref, vt_ref, ot_ref, acc_ref, den_ref, m_ref, *, tk, n_sub, fp8_scores):
    s_len = k_ref.shape[0]
    qt = qt_ref[...]
    qf = qt.astype(F32)
    qn = jnp.sqrt(jnp.sum(qf * qf, axis=0, keepdims=True))
    bound = qn * kmax_ref[0:1, 0:1] * (1.0 + 2.0 ** -6) + 1.0
    if fp8_scores:
        qs = qf[:HD] * Q8_SCALE
        q_hi = qs.astype(FP8)
        q_lo = (qs - q_hi.astype(F32)).astype(FP8)
        tail_row = lax.broadcasted_iota(jnp.int32, (HD, qt.shape[1]), 0)
        neg_bound = -jnp.minimum(bound * (1.0 + 2.0 ** -2), FP8_MAX)
        q8 = jnp.concatenate([q_hi, q_hi, q_lo, jnp.zeros_like(q_hi)], axis=0)
        q8a = jnp.concatenate([q_hi, q_hi, q_lo, jnp.where(tail_row == 0, neg_bound, 0.0).astype(FP8)], axis=0)
        raw_score = lambda sl: jnp.dot(k_ref[sl, :], q8, preferred_element_type=F32)
        shifted_score = lambda sl: jnp.dot(k_ref[sl, :], q8a, preferred_element_type=F32)
    else:
        row = lax.broadcasted_iota(jnp.int32, qt.shape, 0)
        qa = jnp.where(row == QK_PAD - 1, -bound, qf).astype(BF16)
        raw_score = lambda sl: jnp.dot(k_ref[sl, :], qt, preferred_element_type=F32)
        shifted_score = lambda sl: jnp.dot(k_ref[sl, :], qa, preferred_element_type=F32)

    acc_ref[...] = jnp.zeros_like(acc_ref)
    den_ref[...] = jnp.zeros_like(den_ref)
    span = tk * n_sub

    def bounded(jj, carry):
        off = jj * span
        chunk = lambda u: pl.ds(pl.multiple_of(off + u * tk, tk), tk)
        score = lambda u: shifted_score(chunk(u))
        ahead = SCORE_LOOKAHEAD_FP8 if fp8_scores else SCORE_LOOKAHEAD
        st = [score(u) for u in range(min(ahead, n_sub))]
        tot = den = None
        for u in range(n_sub):
            if u + ahead < n_sub:
                st.append(score(u + ahead))
            p = jnp.exp2(st[u])
            pv = jnp.dot(vt_ref[:, chunk(u)], p.astype(BF16), preferred_element_type=F32)
            ps = jnp.sum(p, axis=0, keepdims=True)
            tot = pv if tot is None else tot + pv
            den = ps if den is None else den + ps
        acc_ref[...] += tot
        den_ref[...] += den
        return carry

    lax.fori_loop(0, s_len // span, bounded, 0)
    den = den_ref[...]
    trusted = (jnp.min(den) >= DENOM_FLOOR) & (jnp.max(den) <= DENOM_CEIL)

    @pl.when(jnp.logical_not(trusted))
    def _():
        m_ref[...] = jnp.full_like(m_ref, NEG_INF)
        acc_ref[...] = jnp.zeros_like(acc_ref)
        den_ref[...] = jnp.zeros_like(den_ref)

        def running_max(j, carry):
            o = pl.multiple_of(j * tk, tk)
            st = raw_score(pl.ds(o, tk))
            m_old = m_ref[...]
            m_new = jnp.maximum(m_old, jnp.max(st, axis=0, keepdims=True))
            alpha = jnp.exp2(m_old - m_new)
            p = jnp.exp2(st - m_new)
            acc_ref[...] = alpha * acc_ref[...] + jnp.dot(
                vt_ref[:, pl.ds(o, tk)], p.astype(BF16), preferred_element_type=F32)
            den_ref[...] = alpha * den_ref[...] + jnp.sum(p, axis=0, keepdims=True)
            m_ref[...] = m_new
            return carry

        lax.fori_loop(0, s_len // tk, running_max, 0)

    ot_ref[...] = (acc_ref[...] / den_ref[...]).astype(ot_ref.dtype)


def _attention(qt, k, vt, knorm, tq=512, tk=256, n_sub=64):
    b, hq, _, s = qt.shape
    hk = k.shape[1]
    rep = hq // hk
    tq = min(tq, s)
    tk = min(tk, s)
    n_sub = min(n_sub, s // tk)
    kmax = jnp.broadcast_to(jnp.max(knorm, axis=(2, 3))[:, :, None, None], (b, hk, 8, 128))
    return pl.pallas_call(
        functools.partial(_attn_kernel, tk=tk, n_sub=n_sub, fp8_scores=k.dtype == FP8),
        grid=(b, hq, s // tq),
        in_specs=[pl.BlockSpec((None, None, 8, 128), lambda bi, h, i: (bi, h // rep, 0, 0)),
                  pl.BlockSpec((None, None, QK_PAD, tq), lambda bi, h, i: (bi, h, 0, i)),
                  pl.BlockSpec((None, None, s, k.shape[3]), lambda bi, h, i: (bi, h // rep, 0, 0)),
                  pl.BlockSpec((None, None, V_A, s), lambda bi, h, i: (bi, h // rep, 0, 0))],
        out_specs=pl.BlockSpec((None, None, V_A, tq), lambda bi, h, i: (bi, h, 0, i)),
        out_shape=jax.ShapeDtypeStruct((b, hq, V_A, s), BF16),
        scratch_shapes=[pltpu.VMEM((V_A, tq), F32), pltpu.VMEM((1, tq), F32), pltpu.VMEM((1, tq), F32)],
        compiler_params=_cparams(("parallel", "parallel", "arbitrary")),
        name="attention",
    )(kmax, qt, k, vt)


def _dilated_kernel(*refs, n_t):
    n_in = 7 * N_GROUPS_B
    bias_ref, o_ref, kbuf, vbuf, os_ref, ls_ref = refs[n_in:]
    tile = o_ref.shape[0]
    rad = BAND_R
    ti = pl.program_id(1)
    col = lax.broadcasted_iota(jnp.int32, (1, 3 * rad), 1)
    edge_lo = jnp.where((col < rad) & (ti == 0), NEG_INF, 0.0)
    edge_hi = jnp.where((col >= 2 * rad) & (ti == n_t - 1), NEG_INF, 0.0)
    head_of_lane = lax.broadcasted_iota(jnp.int32, (1, GROUP_B_COLS), 1) // HD
    for g, (_, dil) in enumerate(DIL_PAIRS):
        q_ref, kp_ref, kc_ref, kn_ref, vp_ref, vc_ref, vn_ref = refs[7 * g:7 * g + 7]
        lblk = tile // dil
        n_blk = lblk // rad

        def class_rows(r, start, size):
            return pl.ds(start, size) if dil == 1 else pl.ds(r + dil * start, size, stride=dil)

        def rows(ref, r, start, size):
            sl = class_rows(r, start, size)
            return jnp.concatenate([ref[half, sl, :] for half in range(LANE_HALVES)], axis=1)

        for r in range(dil):
            for buf, p_ref, c_ref, n_ref in ((kbuf, kp_ref, kc_ref, kn_ref), (vbuf, vp_ref, vc_ref, vn_ref)):
                buf[0:rad] = rows(p_ref, r, 0, rad).astype(BF16)
                buf[rad:rad + lblk] = rows(c_ref, r, 0, lblk).astype(BF16)
                buf[rad + lblk:2 * rad + lblk] = rows(n_ref, r, 0, rad).astype(BF16)
            for c in range(n_blk):
                q = (rows(q_ref, r, c * rad, rad) * (HD ** -0.5)).astype(BF16)
                kw = kbuf[c * rad:(c + 3) * rad, :]
                vw = vbuf[c * rad:(c + 3) * rad, :]
                qs = jnp.concatenate([jnp.where(head_of_lane == h, q, jnp.zeros_like(q)) for h in range(HPG_B)],
                                     axis=0)
                s = lax.dot_general(qs, kw, (((1,), (1,)), ((), ())), preferred_element_type=F32)
                s = s + bias_ref[g]
                if c == 0:
                    s = s + edge_lo
                if c == n_blk - 1:
                    s = s + edge_hi
                m = jnp.max(s, axis=-1, keepdims=True)
                p = jnp.exp(s - m)
                l = jnp.sum(p, axis=-1, keepdims=True)
                ov = jnp.dot((p / l).astype(BF16), vw, preferred_element_type=F32)
                lse = m + jnp.log(l)
                o_acc = ov[(HPG_B - 1) * rad:]
                lse_acc = jnp.broadcast_to(lse[(HPG_B - 1) * rad:], (rad, GROUP_B_COLS))
                for h in range(HPG_B - 1):
                    sel = head_of_lane == h
                    o_acc = jnp.where(sel, ov[h * rad:(h + 1) * rad], o_acc)
                    lse_acc = jnp.where(sel, lse[h * rad:(h + 1) * rad], lse_acc)
                dst = class_rows(r, c * rad, rad)
                for half in range(LANE_HALVES):
                    os_ref[g, half, dst, :] = o_acc[:, half * LANES:(half + 1) * LANES]
                    ls_ref[g, half, dst, :] = lse_acc[:, half * LANES:(half + 1) * LANES]

    for half in range(LANE_HALVES):
        l0, l1, l2 = ls_ref[0, half], ls_ref[1, half], ls_ref[2, half]
        m = jnp.maximum(jnp.maximum(l0, l1), l2)
        e0, e1, e2 = jnp.exp(l0 - m), jnp.exp(l1 - m), jnp.exp(l2 - m)
        inv = 1.0 / (e0 + e1 + e2)
        o_ref[:, half * LANES:(half + 1) * LANES] = (
            (e0 * inv) * os_ref[0, half] + (e1 * inv) * os_ref[1, half] + (e2 * inv) * os_ref[2, half]
        ).astype(o_ref.dtype)


def _dilated(qkvb, bias):
    b, _, s, _ = qkvb.shape
    tile = DIL_TILE
    n_t = s // tile
    in_specs = []
    for g, (_, dil) in enumerate(DIL_PAIRS):
        halo = BAND_R * dil
        per_tile = tile // halo
        n_halo = s // halo

        def cur(part, g=g):
            return pl.BlockSpec((None, LANE_HALVES, tile, LANES), lambda bi, i: (bi, part * N_GROUPS_B + g, i, 0))

        def prev(part, g=g, halo=halo, per_tile=per_tile):
            return pl.BlockSpec((None, LANE_HALVES, halo, LANES),
                                lambda bi, i: (bi, part * N_GROUPS_B + g, jnp.maximum(i * per_tile - 1, 0), 0))

        def nxt(part, g=g, halo=halo, per_tile=per_tile, n_halo=n_halo):
            return pl.BlockSpec((None, LANE_HALVES, halo, LANES),
                                lambda bi, i: (bi, part * N_GROUPS_B + g, jnp.minimum((i + 1) * per_tile, n_halo - 1), 0))

        in_specs += [cur(0), prev(1), cur(1), nxt(1), prev(2), cur(2), nxt(2)]
    in_specs.append(pl.BlockSpec(bias.shape, lambda bi, i: (0, 0, 0)))
    return pl.pallas_call(
        functools.partial(_dilated_kernel, n_t=n_t),
        grid=(b, n_t),
        in_specs=in_specs,
        out_specs=pl.BlockSpec((None, tile, GROUP_B_COLS), lambda bi, i: (bi, i, 0)),
        out_shape=jax.ShapeDtypeStruct((b, s, GROUP_B_COLS), BF16),
        scratch_shapes=[pltpu.VMEM((tile + 2 * BAND_R, GROUP_B_COLS), BF16),
                        pltpu.VMEM((tile + 2 * BAND_R, GROUP_B_COLS), BF16),
                        pltpu.VMEM((N_GROUPS_B, LANE_HALVES, tile, LANES), F32),
                        pltpu.VMEM((N_GROUPS_B, LANE_HALVES, tile, LANES), F32)],
        compiler_params=_cparams(("parallel", "parallel")),
        name="dilated",
    )(*([qkvb] * (7 * N_GROUPS_B)), bias)


def _merge_kernel(x_ref, gmix_ref, oat_ref, oct_ref, ob_ref, wg_ref, bg_ref, woa_ref, wob_ref, woc_ref, wout_ref,
                  o_ref):
    x = x_ref[...]
    ub = _rms_rows(x, gmix_ref[...]).astype(BF16)
    d = x.shape[1]
    tn = (((0,), (0,)), ((), ()))
    ya = lax.dot_general(oat_ref[...], woa_ref[...], tn, preferred_element_type=F32)
    yb = jnp.dot(ob_ref[...], wob_ref[...], preferred_element_type=F32)
    yc = lax.dot_general(oct_ref[...], woc_ref[...], tn, preferred_element_type=F32)

    def gate(n):
        z = jnp.dot(ub, wg_ref[:, n * d:(n + 1) * d], preferred_element_type=F32) + bg_ref[:, n * d:(n + 1) * d]
        return jax.nn.sigmoid(z)

    merged = gate(0) * ya + gate(1) * yb + gate(2) * yc
    o_ref[...] = x + jnp.dot(merged.astype(BF16), wout_ref[...], preferred_element_type=F32)


def _merge(x3d, oat, oct, ob, lw, ts=512):
    b, s, d = x3d.shape
    ts = min(ts, s)
    tok = lambda w: pl.BlockSpec((None, ts, w), lambda bi, i: (bi, i, 0))
    feat = lambda r: pl.BlockSpec((None, r, ts), lambda bi, i: (bi, 0, i))
    consts = [lw["w_gate"], lw["b_gate"], lw["w_oa"], lw["w_ob"], lw["w_oc"], lw["w_out"]]
    return pl.pallas_call(
        _merge_kernel,
        grid=(b, s // ts),
        in_specs=[tok(d), _const_spec(lw["g_mix"].shape), feat(HA * V_A), feat(HC * HD), tok(GROUP_B_COLS)]
                 + [_const_spec(c.shape) for c in consts],
        out_specs=tok(d),
        out_shape=jax.ShapeDtypeStruct((b, s, d), F32),
        compiler_params=_cparams(("parallel", "parallel")),
        name="merge",
    )(x3d, lw["g_mix"], oat.reshape(b, HA * V_A, s), oct.reshape(b, HC * HD, s), ob, *consts)


def _t5_bucket(rel):
    nb = T5_BUCKETS // 2
    max_exact = nb // 2
    n = np.abs(rel)
    large = max_exact + (np.log(np.maximum(n, 1) / max_exact) / math.log(T5_MAX_DIST / max_exact)
                         * (nb - max_exact)).astype(np.int32)
    large = np.minimum(large, nb - 1)
    return (rel > 0).astype(np.int32) * nb + np.where(n < max_exact, n, large).astype(np.int32)


def _band_bias(rel_bias):
    qi = np.arange(BAND_R)[:, None]
    kj = np.arange(3 * BAND_R)[None, :]
    rel = kj - BAND_R - qi
    band = np.where(np.abs(rel) <= BAND_R, 0.0, NEG_INF).astype(np.float32)
    out = []
    for g, (_, dil) in enumerate(DIL_PAIRS):
        tab = rel_bias[:, g * HPG_B:(g + 1) * HPG_B]
        bias = jnp.transpose(tab[_t5_bucket(rel * dil)], (2, 0, 1)).astype(F32)
        out.append((bias + band[None]).reshape(HPG_B * BAND_R, 3 * BAND_R))
    return jnp.stack(out, axis=0)


def _rope_tables(s):
    freqs = (ROPE_THETA ** (-np.arange(ROPE_HALF) / ROPE_HALF)).astype(np.float32)
    pos = jnp.arange(s, dtype=jnp.int32)
    rows = jnp.repeat(jnp.arange(s // GRID_W, dtype=jnp.int32), GRID_W)
    cols = pos % GRID_W
    tabs = []
    for p in (pos, rows, cols):
        ang = p.astype(F32)[:, None] * jnp.asarray(freqs)[None, :]
        tabs += [jnp.cos(ang).T, jnp.sin(ang).T]
    return jnp.stack(tabs, axis=0)


def _ffn_weights(w_in, w_out):
    d = w_in.shape[0]
    n_chunks = D_FF // FFN_CHUNK
    w_in_c = w_in.astype(BF16).reshape(d, 2 * n_chunks, FFN_CHUNK).transpose(1, 0, 2)
    w_out_c = w_out.astype(BF16).reshape(n_chunks, FFN_CHUNK, d)
    return w_in_c, w_out_c


def _layer_weights(W, i):
    row = lambda v: v.reshape(1, -1).astype(F32)
    colv = lambda v: v.reshape(-1, 1).astype(F32)
    w_in = W["w_in"][i]
    w_a = w_in[:, :A_COLS]
    w_b = w_in[:, A_COLS:A_COLS + B_COLS]
    w_c = w_in[:, A_COLS + B_COLS:]
    head = np.arange(GROUP_B_COLS) // HD
    ones_blk = jnp.asarray((head[:, None] == head[None, :]).astype(np.float32) / HD, dtype=BF16)
    lw = dict(
        g_ffn1=row(W["g_ffn1"][i]), g_ffn2=row(W["g_ffn2"][i]), g_mix=row(W["g_mix"][i]), g_ple=row(W["g_ple"][i]),
        w_b=w_b.astype(BF16),
        w_act=jnp.concatenate([w_a, w_c], axis=1).T.astype(BF16),
        ones_blk=ones_blk,
        g_qb=row(jnp.tile(W["g_qb"][i], HPG_B)), g_kb=row(jnp.tile(W["g_kb"][i], HPG_B)),
        g_cq=colv(W["g_cq"][i]), g_ckv=colv(W["g_ckv"][i]),
        w_uqt=W["w_uq"][i].T.astype(BF16), w_ukvt=W["w_ukv"][i].T.astype(BF16),
        g_qa=colv(W["g_qa"][i]), g_ka=colv(W["g_ka"][i]), g_qc=colv(W["g_qc"][i]), g_kc=colv(W["g_kc"][i]),
        w_gate=W["w_gate"][i].astype(BF16), b_gate=row(W["b_gate"][i]),
        w_oa=W["w_oa"][i].astype(BF16), w_ob=W["w_ob"][i].astype(BF16), w_oc=W["w_oc"][i].astype(BF16),
        w_out=W["w_out"][i].astype(BF16),
        w_pg=W["w_pg"][i].astype(BF16), w_ple=W["w_ple"][i].astype(BF16),
    )
    lw["ffn1"] = _ffn_weights(W["w_ffn1_in"][i], W["w_ffn1_out"][i])
    lw["ffn2"] = _ffn_weights(W["w_ffn2_in"][i], W["w_ffn2_out"][i])
    return lw


def _trunk(x, p, layers, bias):
    b, s, d = x.shape
    tab = _rope_tables(s)
    for i, lw in enumerate(layers):
        x1 = _ffn(x.reshape(b * s, d), lw["g_ffn1"], *lw["ffn1"]).reshape(b, s, d)
        qkvb, qat, ka, vat, kna, qct, kc, vct, knc = _proj(x1, lw, tab)
        oat = _attention(qat, ka, vat, kna)
        oct = _attention(qct, kc, vct, knc)
        x2 = _merge(x1, oat, oct, _dilated(qkvb, bias), lw)
        x = _ffn(x2.reshape(b * s, d), lw["g_ffn2"], *lw["ffn2"],
                 ple_args=(p[i].reshape(b * s, PLE_DIM), lw["g_ple"], lw["w_pg"], lw["w_ple"])).reshape(b, s, d)
    return x


def kernel(x_prompt, x_sample, p_prompt, p_sample, g_ffn1, w_ffn1_in, w_ffn1_out, g_mix, w_in, g_cq, g_ckv, w_uq,
           w_ukv, g_qa, g_ka, g_qb, g_kb, rel_bias, g_qc, g_kc, w_gate, b_gate, w_oa, w_ob, w_oc, w_out, g_ffn2,
           w_ffn2_in, w_ffn2_out, g_ple, w_pg, w_ple):
    W = dict(g_ffn1=g_ffn1, w_ffn1_in=w_ffn1_in, w_ffn1_out=w_ffn1_out, g_mix=g_mix, w_in=w_in,
             g_cq=g_cq, g_ckv=g_ckv, w_uq=w_uq, w_ukv=w_ukv, g_qa=g_qa, g_ka=g_ka,
             g_qb=g_qb, g_kb=g_kb, g_qc=g_qc, g_kc=g_kc,
             w_gate=w_gate, b_gate=b_gate, w_oa=w_oa, w_ob=w_ob, w_oc=w_oc, w_out=w_out,
             g_ffn2=g_ffn2, w_ffn2_in=w_ffn2_in, w_ffn2_out=w_ffn2_out,
             g_ple=g_ple, w_pg=w_pg, w_ple=w_ple)
    layers = [_layer_weights(W, i) for i in range(g_mix.shape[0])]
    bias = _band_bias(rel_bias)
    return (_trunk(x_prompt, p_prompt, layers, bias), _trunk(x_sample, p_sample, layers, bias))
```

```python
import functools
import math

import numpy as np
import jax
import jax.numpy as jnp
from jax import lax
from jax.experimental import pallas as pl
from jax.experimental.pallas import tpu as pltpu

F32 = jnp.float32
BF16 = jnp.bfloat16

D_MODEL = 1024
GRID_W = 64
PLE_DIM = 256
D_FF = 2816
NORM_EPS = 1e-6
ROPE_THETA = 10000.0
NEG_INF = -1e30

HA = 8
Q_RANK = 256
KV_RANK = 128
NOPE_A = 64
ROPE_A = 32
V_A = 64
QK_A = NOPE_A + ROPE_A

HD = 64
DIL_PAIRS = ((128, 1), (512, 4), (2048, 16))
N_GROUPS_B = 3
HPG_B = 4
N_HEADS_B = N_GROUPS_B * HPG_B
T5_BUCKETS = 32
T5_MAX_DIST = 1024
BAND_R = 64
DIL_TILE = BAND_R * max(d for _, d in DIL_PAIRS)

HC = 8
KVC = 2
GC = HC // KVC

A_COLS = Q_RANK + KV_RANK + ROPE_A
B_COLS = 3 * N_HEADS_B * HD
C_COLS = (HC + 2 * KVC) * HD
GROUP_B_COLS = HPG_B * HD
LANES = 128
LANE_HALVES = GROUP_B_COLS // LANES

ROPE_HALF = 16
QK_PAD = 128
LOG2E = math.log2(math.e)

VMEM_LIMIT_BYTES = 56 * 1024 * 1024
FFN_CHUNK = 256
SCORE_LOOKAHEAD = 2
DENOM_FLOOR = 2.0 ** -80


def _cparams(semantics):
    return pltpu.CompilerParams(dimension_semantics=semantics, vmem_limit_bytes=VMEM_LIMIT_BYTES)


def _const_spec(shape):
    nd = len(shape)
    return pl.BlockSpec(shape, lambda *_: (0,) * nd, pipeline_mode=pl.Buffered(1))


def _rms_rows(x, g):
    ms = jnp.mean(x * x, axis=-1, keepdims=True)
    return x * lax.rsqrt(ms + NORM_EPS) * g


def _rms_cols(x, g):
    ms = jnp.mean(x * x, axis=0, keepdims=True)
    return x * lax.rsqrt(ms + NORM_EPS) * g


def _rope_cols(x, cos, sin):
    x1, x2 = x[:ROPE_HALF], x[ROPE_HALF:]
    return x1 * cos - x2 * sin, x1 * sin + x2 * cos


def _ffn_kernel(*refs, n_chunks, ple):
    if ple:
        x_ref, g_ref, win_ref, wout_ref, pe_ref, gple_ref, wpg_ref, wple_ref, o_ref, acc_ref = refs
    else:
        x_ref, g_ref, win_ref, wout_ref, o_ref, acc_ref = refs
    x = x_ref[...]
    u = _rms_rows(x, g_ref[...]).astype(BF16)
    d_ff = wout_ref.shape[0]
    for c in range(n_chunks):
        lo, hi = c * FFN_CHUNK, (c + 1) * FFN_CHUNK
        a = jnp.dot(u, win_ref[:, lo:hi], preferred_element_type=F32)
        b = jnp.dot(u, win_ref[:, d_ff + lo:d_ff + hi], preferred_element_type=F32)
        h = (a * jax.nn.sigmoid(a) * b).astype(BF16)
        d = jnp.dot(h, wout_ref[lo:hi, :], preferred_element_type=F32)
        if c == 0:
            acc_ref[...] = d
        elif c < n_chunks - 1:
            acc_ref[...] += d
        else:
            y = x + 0.5 * (acc_ref[...] + d)
    if ple:
        gate = jax.nn.sigmoid(
            jnp.dot(_rms_rows(y, gple_ref[...]).astype(BF16), wpg_ref[...], preferred_element_type=F32))
        y = y + gate * jnp.dot(pe_ref[...].astype(BF16), wple_ref[...], preferred_element_type=F32)
    o_ref[...] = y


def _ffn(x2d, g, w_in_c, w_out_c, ple_args=None, tm=1024):
    n, d = x2d.shape
    n_chunks = w_out_c.shape[0] // FFN_CHUNK
    tm = min(tm, n)
    row = lambda i: (i, 0)
    in_specs = [pl.BlockSpec((tm, d), row), _const_spec(g.shape), _const_spec(w_in_c.shape), _const_spec(w_out_c.shape)]
    args = [x2d, g, w_in_c, w_out_c]
    if ple_args is not None:
        pe2d, g_ple, w_pg, w_ple = ple_args
        in_specs += [pl.BlockSpec((tm, pe2d.shape[1]), row), _const_spec(g_ple.shape), _const_spec(w_pg.shape),
                     _const_spec(w_ple.shape)]
        args += [pe2d, g_ple, w_pg, w_ple]
    return pl.pallas_call(
        functools.partial(_ffn_kernel, n_chunks=n_chunks, ple=ple_args is not None),
        grid=(n // tm,),
        in_specs=in_specs,
        out_specs=pl.BlockSpec((tm, d), row),
        out_shape=jax.ShapeDtypeStruct((n, d), F32),
        scratch_shapes=[pltpu.VMEM((tm, d), F32)],
        compiler_params=_cparams(("parallel",)),
        name="ffn_ple" if ple_args is not None else "ffn",
    )(*args)


_T_CQ = 0
_T_CKV = Q_RANK
_T_KR = Q_RANK + KV_RANK
_T_QC = A_COLS
_T_KC = A_COLS + HC * HD
_T_VC = A_COLS + (HC + KVC) * HD
_T_ROWS = A_COLS + C_COLS


def _proj_kernel(x_ref, gmix_ref, wb_ref, wact_ref, ones_ref, gqb_ref, gkb_ref, gcq_ref, gckv_ref, wuqt_ref,
                 wukvt_ref, gqa_ref, gka_ref, gqc_ref, gkc_ref, tab_ref,
                 qkvb_ref, qat_ref, ka_ref, vat_ref, kna_ref, qct_ref, kc_ref, vct_ref, knc_ref, *, scale_a, scale_c):
    ts = x_ref.shape[0]

    def key_tail(rows):
        r = lax.broadcasted_iota(jnp.int32, (rows, ts), 0)
        return jnp.where(r == rows - 1, 1.0, 0.0).astype(F32)

    def key_norm(kt):
        return jnp.sqrt(jnp.sum(kt * kt, axis=0, keepdims=True))

    ub = _rms_rows(x_ref[...], gmix_ref[...]).astype(BF16)

    pb = jnp.dot(ub, wb_ref[...], preferred_element_type=F32)
    ones = ones_ref[...]
    for part, g_ref in ((0, gqb_ref), (1, gkb_ref), (2, None)):
        for c in range(N_GROUPS_B):
            blk = part * N_GROUPS_B + c
            xx = pb[:, blk * GROUP_B_COLS:(blk + 1) * GROUP_B_COLS]
            if g_ref is not None:
                ms = jnp.dot((xx * xx).astype(BF16), ones, preferred_element_type=F32)
                xx = xx * lax.rsqrt(ms + NORM_EPS) * g_ref[...]
            for half in range(LANE_HALVES):
                qkvb_ref[LANE_HALVES * blk + half] = xx[:, half * LANES:(half + 1) * LANES]

    pt = lax.dot_general(wact_ref[...], ub, (((1,), (1,)), ((), ())), preferred_element_type=F32)
    cos_p, sin_p, cos_r, sin_r, cos_c, sin_c = (tab_ref[i] for i in range(6))

    cq = _rms_cols(pt[_T_CQ:_T_CQ + Q_RANK], gcq_ref[...]).astype(BF16)
    qa = jnp.dot(wuqt_ref[...], cq, preferred_element_type=F32)
    ckv = _rms_cols(pt[_T_CKV:_T_CKV + KV_RANK], gckv_ref[...]).astype(BF16)
    kv = jnp.dot(wukvt_ref[...], ckv, preferred_element_type=F32)
    gqa = gqa_ref[...]
    gka = gka_ref[...]
    kr1, kr2 = _rope_cols(_rms_cols(pt[_T_KR:_T_KR + ROPE_A], gka[NOPE_A:]), cos_p, sin_p)
    k_tail = key_tail(QK_PAD - QK_A)
    kr_sq = jnp.sum(kr1 * kr1 + kr2 * kr2, axis=0, keepdims=True)
    for h in range(HA):
        q0 = h * QK_A
        qn = _rms_cols(qa[q0:q0 + NOPE_A], gqa[:NOPE_A])
        q1, q2 = _rope_cols(_rms_cols(qa[q0 + NOPE_A:q0 + QK_A], gqa[NOPE_A:]), cos_p, sin_p)
        qat_ref[h, 0:NOPE_A] = (qn * scale_a).astype(BF16)
        qat_ref[h, NOPE_A:NOPE_A + ROPE_HALF] = (q1 * scale_a).astype(BF16)
        qat_ref[h, NOPE_A + ROPE_HALF:QK_A] = (q2 * scale_a).astype(BF16)
        qat_ref[h, QK_A:QK_PAD] = jnp.zeros((QK_PAD - QK_A, ts), BF16)
        k0 = h * (NOPE_A + V_A)
        kn = _rms_cols(kv[k0:k0 + NOPE_A], gka[:NOPE_A])
        kt = jnp.concatenate([kn, kr1, kr2, k_tail], axis=0)
        ka_ref[h] = kt.T.astype(BF16)
        kna_ref[h] = jnp.sqrt(jnp.sum(kn * kn, axis=0, keepdims=True) + kr_sq)
        vat_ref[h] = kv[k0 + NOPE_A:k0 + NOPE_A + V_A].astype(BF16)

    def axial(t):
        a1, a2 = _rope_cols(t[:HD // 2], cos_r, sin_r)
        b1, b2 = _rope_cols(t[HD // 2:], cos_c, sin_c)
        return a1, a2, b1, b2

    gqc = gqc_ref[...]
    gkc = gkc_ref[...]
    for h in range(HC):
        parts = axial(_rms_cols(pt[_T_QC + h * HD:_T_QC + (h + 1) * HD], gqc) * scale_c)
        qct_ref[h, 0:HD] = jnp.concatenate(parts, axis=0).astype(BF16)
        qct_ref[h, HD:QK_PAD] = jnp.zeros((QK_PAD - HD, ts), BF16)
    c_tail = key_tail(QK_PAD - HD)
    for h in range(KVC):
        kh = jnp.concatenate(axial(_rms_cols(pt[_T_KC + h * HD:_T_KC + (h + 1) * HD], gkc)), axis=0)
        kt = jnp.concatenate([kh, c_tail], axis=0)
        kc_ref[h] = kt.T.astype(BF16)
        knc_ref[h] = key_norm(kh)
        vct_ref[h] = pt[_T_VC + h * HD:_T_VC + (h + 1) * HD].astype(BF16)


def _proj(x3d, lw, tab, ts=512):
    b, s, d = x3d.shape
    ts = min(ts, s)
    consts = [lw["g_mix"], lw["w_b"], lw["w_act"], lw["ones_blk"], lw["g_qb"], lw["g_kb"], lw["g_cq"], lw["g_ckv"],
              lw["w_uqt"], lw["w_ukvt"], lw["g_qa"], lw["g_ka"], lw["g_qc"], lw["g_kc"]]
    in_specs = ([pl.BlockSpec((None, ts, d), lambda bi, i: (bi, i, 0))] + [_const_spec(c.shape) for c in consts]
                + [pl.BlockSpec((6, ROPE_HALF, ts), lambda bi, i: (0, 0, i))])
    tok_major = lambda h: pl.BlockSpec((None, h, ts, QK_PAD), lambda bi, i: (bi, 0, i, 0))
    feat_major = lambda h, r: pl.BlockSpec((None, h, r, ts), lambda bi, i: (bi, 0, 0, i))
    out_specs = [pl.BlockSpec((None, B_COLS // LANES, ts, LANES), lambda bi, i: (bi, 0, i, 0)),
                 feat_major(HA, QK_PAD), tok_major(HA), feat_major(HA, V_A), feat_major(HA, 1),
                 feat_major(HC, QK_PAD), tok_major(KVC), feat_major(KVC, HD), feat_major(KVC, 1)]
    out_shape = [jax.ShapeDtypeStruct((b, B_COLS // LANES, s, LANES), F32),
                 jax.ShapeDtypeStruct((b, HA, QK_PAD, s), BF16), jax.ShapeDtypeStruct((b, HA, s, QK_PAD), BF16),
                 jax.ShapeDtypeStruct((b, HA, V_A, s), BF16), jax.ShapeDtypeStruct((b, HA, 1, s), F32),
                 jax.ShapeDtypeStruct((b, HC, QK_PAD, s), BF16), jax.ShapeDtypeStruct((b, KVC, s, QK_PAD), BF16),
                 jax.ShapeDtypeStruct((b, KVC, HD, s), BF16), jax.ShapeDtypeStruct((b, KVC, 1, s), F32)]
    return pl.pallas_call(
        functools.partial(_proj_kernel, scale_a=QK_A ** -0.5 * LOG2E, scale_c=HD ** -0.5 * LOG2E),
        grid=(b, s // ts),
        in_specs=in_specs,
        out_specs=out_specs,
        out_shape=out_shape,
        compiler_params=_cparams(("parallel", "parallel")),
        name="proj",
    )(x3d, *consts, tab)


def _attn_kernel(kmax_ref, qt_ref, k_ref, vt_ref, ot_ref, acc_ref, den_ref, m_ref, *, tk, n_sub):
    s_len = k_ref.shape[0]
    qt = qt_ref[...]
    qf = qt.astype(F32)
    qn = jnp.sqrt(jnp.sum(qf * qf, axis=0, keepdims=True))
    bound = qn * kmax_ref[0:1, 0:1] * (1.0 + 2.0 ** -6) + 1.0
    row = lax.broadcasted_iota(jnp.int32, qt.shape, 0)
    qa = jnp.where(row == QK_PAD - 1, -bound, qf).astype(BF16)

    acc_ref[...] = jnp.zeros_like(acc_ref)
    den_ref[...] = jnp.zeros_like(den_ref)
    span = tk * n_sub

    def bounded(jj, carry):
        off = jj * span
        chunk = lambda u: pl.ds(pl.multiple_of(off + u * tk, tk), tk)
        score = lambda u: jnp.dot(k_ref[chunk(u), :], qa, preferred_element_type=F32)
        st = [score(u) for u in range(min(SCORE_LOOKAHEAD, n_sub))]
        tot = den = None
        for u in range(n_sub):
            if u + SCORE_LOOKAHEAD < n_sub:
                st.append(score(u + SCORE_LOOKAHEAD))
            p = jnp.exp2(st[u])
            pv = jnp.dot(vt_ref[:, chunk(u)], p.astype(BF16), preferred_element_type=F32)
            ps = jnp.sum(p, axis=0, keepdims=True)
            tot = pv if tot is None else tot + pv
            den = ps if den is None else den + ps
        acc_ref[...] += tot
        den_ref[...] += den
        return carry

    lax.fori_loop(0, s_len // span, bounded, 0)
    trusted = jnp.min(den_ref[...]) >= DENOM_FLOOR

    @pl.when(jnp.logical_not(trusted))
    def _():
        m_ref[...] = jnp.full_like(m_ref, NEG_INF)
        acc_ref[...] = jnp.zeros_like(acc_ref)
        den_ref[...] = jnp.zeros_like(den_ref)

        def running_max(j, carry):
            o = pl.multiple_of(j * tk, tk)
            st = jnp.dot(k_ref[pl.ds(o, tk), :], qt, preferred_element_type=F32)
            m_old = m_ref[...]
            m_new = jnp.maximum(m_old, jnp.max(st, axis=0, keepdims=True))
            alpha = jnp.exp2(m_old - m_new)
            p = jnp.exp2(st - m_new)
            acc_ref[...] = alpha * acc_ref[...] + jnp.dot(
                vt_ref[:, pl.ds(o, tk)], p.astype(BF16), preferred_element_type=F32)
            den_ref[...] = alpha * den_ref[...] + jnp.sum(p, axis=0, keepdims=True)
            m_ref[...] = m_new
            return carry

        lax.fori_loop(0, s_len // tk, running_max, 0)

    ot_ref[...] = (acc_ref[...] / den_ref[...]).astype(ot_ref.dtype)


def _attention(qt, k, vt, knorm, tq=512, tk=256, n_sub=64):
    b, hq, _, s = qt.shape
    hk = k.shape[1]
    rep = hq // hk
    tq = min(tq, s)
    tk = min(tk, s)
    n_sub = min(n_sub, s // tk)
    kmax = jnp.broadcast_to(jnp.max(knorm, axis=(2, 3))[:, :, None, None], (b, hk, 8, 128))
    return pl.pallas_call(
        functools.partial(_attn_kernel, tk=tk, n_sub=n_sub),
        grid=(b, hq, s // tq),
        in_specs=[pl.BlockSpec((None, None, 8, 128), lambda bi, h, i: (bi, h // rep, 0, 0)),
                  pl.BlockSpec((None, None, QK_PAD, tq), lambda bi, h, i: (bi, h, 0, i)),
                  pl.BlockSpec((None, None, s, QK_PAD), lambda bi, h, i: (bi, h // rep, 0, 0)),
                  pl.BlockSpec((None, None, V_A, s), lambda bi, h, i: (bi, h // rep, 0, 0))],
        out_specs=pl.BlockSpec((None, None, V_A, tq), lambda bi, h, i: (bi, h, 0, i)),
        out_shape=jax.ShapeDtypeStruct((b, hq, V_A, s), BF16),
        scratch_shapes=[pltpu.VMEM((V_A, tq), F32), pltpu.VMEM((1, tq), F32), pltpu.VMEM((1, tq), F32)],
        compiler_params=_cparams(("parallel", "parallel", "arbitrary")),
        name="attention",
    )(kmax, qt, k, vt)


def _dilated_kernel(*refs, n_t):
    n_in = 7 * N_GROUPS_B
    bias_ref, o_ref, kbuf, vbuf, os_ref, ls_ref = refs[n_in:]
    tile = o_ref.shape[0]
    rad = BAND_R
    ti = pl.program_id(1)
    col = lax.broadcasted_iota(jnp.int32, (1, 3 * rad), 1)
    edge_lo = jnp.where((col < rad) & (ti == 0), NEG_INF, 0.0)
    edge_hi = jnp.where((col >= 2 * rad) & (ti == n_t - 1), NEG_INF, 0.0)
    head_of_lane = lax.broadcasted_iota(jnp.int32, (1, GROUP_B_COLS), 1) // HD
    for g, (_, dil) in enumerate(DIL_PAIRS):
        q_ref, kp_ref, kc_ref, kn_ref, vp_ref, vc_ref, vn_ref = refs[7 * g:7 * g + 7]
        lblk = tile // dil
        n_blk = lblk // rad

        def class_rows(r, start, size):
            return pl.ds(start, size) if dil == 1 else pl.ds(r + dil * start, size, stride=dil)

        def rows(ref, r, start, size):
            sl = class_rows(r, start, size)
            return jnp.concatenate([ref[half, sl, :] for half in range(LANE_HALVES)], axis=1)

        for r in range(dil):
            for buf, p_ref, c_ref, n_ref in ((kbuf, kp_ref, kc_ref, kn_ref), (vbuf, vp_ref, vc_ref, vn_ref)):
                buf[0:rad] = rows(p_ref, r, 0, rad).astype(BF16)
                buf[rad:rad + lblk] = rows(c_ref, r, 0, lblk).astype(BF16)
                buf[rad + lblk:2 * rad + lblk] = rows(n_ref, r, 0, rad).astype(BF16)
            for c in range(n_blk):
                q = (rows(q_ref, r, c * rad, rad) * (HD ** -0.5)).astype(BF16)
                kw = kbuf[c * rad:(c + 3) * rad, :]
                vw = vbuf[c * rad:(c + 3) * rad, :]
                qs = jnp.concatenate([jnp.where(head_of_lane == h, q, jnp.zeros_like(q)) for h in range(HPG_B)],
                                     axis=0)
                s = lax.dot_general(qs, kw, (((1,), (1,)), ((), ())), preferred_element_type=F32)
                s = s + bias_ref[g]
                if c == 0:
                    s = s + edge_lo
                if c == n_blk - 1:
                    s = s + edge_hi
                m = jnp.max(s, axis=-1, keepdims=True)
                p = jnp.exp(s - m)
                l = jnp.sum(p, axis=-1, keepdims=True)
                ov = jnp.dot((p / l).astype(BF16), vw, preferred_element_type=F32)
                lse = m + jnp.log(l)
                o_acc = ov[(HPG_B - 1) * rad:]
                lse_acc = jnp.broadcast_to(lse[(HPG_B - 1) * rad:], (rad, GROUP_B_COLS))
                for h in range(HPG_B - 1):
                    sel = head_of_lane == h
                    o_acc = jnp.where(sel, ov[h * rad:(h + 1) * rad], o_acc)
                    lse_acc = jnp.where(sel, lse[h * rad:(h + 1) * rad], lse_acc)
                dst = class_rows(r, c * rad, rad)
                for half in range(LANE_HALVES):
                    os_ref[g, half, dst, :] = o_acc[:, half * LANES:(half + 1) * LANES]
                    ls_ref[g, half, dst, :] = lse_acc[:, half * LANES:(half + 1) * LANES]

    for half in range(LANE_HALVES):
        l0, l1, l2 = ls_ref[0, half], ls_ref[1, half], ls_ref[2, half]
        m = jnp.maximum(jnp.maximum(l0, l1), l2)
        e0, e1, e2 = jnp.exp(l0 - m), jnp.exp(l1 - m), jnp.exp(l2 - m)
        inv = 1.0 / (e0 + e1 + e2)
        o_ref[:, half * LANES:(half + 1) * LANES] = (
            (e0 * inv) * os_ref[0, half] + (e1 * inv) * os_ref[1, half] + (e2 * inv) * os_ref[2, half]
        ).astype(o_ref.dtype)


def _dilated(qkvb, bias):
    b, _, s, _ = qkvb.shape
    tile = DIL_TILE
    n_t = s // tile
    in_specs = []
    for g, (_, dil) in enumerate(DIL_PAIRS):
        halo = BAND_R * dil
        per_tile = tile // halo
        n_halo = s // halo

        def cur(part, g=g):
            return pl.BlockSpec((None, LANE_HALVES, tile, LANES), lambda bi, i: (bi, part * N_GROUPS_B + g, i, 0))

        def prev(part, g=g, halo=halo, per_tile=per_tile):
            return pl.BlockSpec((None, LANE_HALVES, halo, LANES),
                                lambda bi, i: (bi, part * N_GROUPS_B + g, jnp.maximum(i * per_tile - 1, 0), 0))

        def nxt(part, g=g, halo=halo, per_tile=per_tile, n_halo=n_halo):
            return pl.BlockSpec((None, LANE_HALVES, halo, LANES),
                                lambda bi, i: (bi, part * N_GROUPS_B + g, jnp.minimum((i + 1) * per_tile, n_halo - 1), 0))

        in_specs += [cur(0), prev(1), cur(1), nxt(1), prev(2), cur(2), nxt(2)]
    in_specs.append(pl.BlockSpec(bias.shape, lambda bi, i: (0, 0, 0)))
    return pl.pallas_call(
        functools.partial(_dilated_kernel, n_t=n_t),
        grid=(b, n_t),
        in_specs=in_specs,
        out_specs=pl.BlockSpec((None, tile, GROUP_B_COLS), lambda bi, i: (bi, i, 0)),
        out_shape=jax.ShapeDtypeStruct((b, s, GROUP_B_COLS), BF16),
        scratch_shapes=[pltpu.VMEM((tile + 2 * BAND_R, GROUP_B_COLS), BF16),
                        pltpu.VMEM((tile + 2 * BAND_R, GROUP_B_COLS), BF16),
                        pltpu.VMEM((N_GROUPS_B, LANE_HALVES, tile, LANES), F32),
                        pltpu.VMEM((N_GROUPS_B, LANE_HALVES, tile, LANES), F32)],
        compiler_params=_cparams(("parallel", "parallel")),
        name="dilated",
    )(*([qkvb] * (7 * N_GROUPS_B)), bias)


def _merge_kernel(x_ref, gmix_ref, oat_ref, oct_ref, ob_ref, wg_ref, bg_ref, woa_ref, wob_ref, woc_ref, wout_ref,
                  o_ref):
    x = x_ref[...]
    ub = _rms_rows(x, gmix_ref[...]).astype(BF16)
    d = x.shape[1]
    tn = (((0,), (0,)), ((), ()))
    ya = lax.dot_general(oat_ref[...], woa_ref[...], tn, preferred_element_type=F32)
    yb = jnp.dot(ob_ref[...], wob_ref[...], preferred_element_type=F32)
    yc = lax.dot_general(oct_ref[...], woc_ref[...], tn, preferred_element_type=F32)

    def gate(n):
        z = jnp.dot(ub, wg_ref[:, n * d:(n + 1) * d], preferred_element_type=F32) + bg_ref[:, n * d:(n + 1) * d]
        return jax.nn.sigmoid(z)

    merged = gate(0) * ya + gate(1) * yb + gate(2) * yc
    o_ref[...] = x + jnp.dot(merged.astype(BF16), wout_ref[...], preferred_element_type=F32)


def _merge(x3d, oat, oct, ob, lw, ts=512):
    b, s, d = x3d.shape
    ts = min(ts, s)
    tok = lambda w: pl.BlockSpec((None, ts, w), lambda bi, i: (bi, i, 0))
    feat = lambda r: pl.BlockSpec((None, r, ts), lambda bi, i: (bi, 0, i))
    consts = [lw["w_gate"], lw["b_gate"], lw["w_oa"], lw["w_ob"], lw["w_oc"], lw["w_out"]]
    return pl.pallas_call(
        _merge_kernel,
        grid=(b, s // ts),
        in_specs=[tok(d), _const_spec(lw["g_mix"].shape), feat(HA * V_A), feat(HC * HD), tok(GROUP_B_COLS)]
                 + [_const_spec(c.shape) for c in consts],
        out_specs=tok(d),
        out_shape=jax.ShapeDtypeStruct((b, s, d), F32),
        compiler_params=_cparams(("parallel", "parallel")),
        name="merge",
    )(x3d, lw["g_mix"], oat.reshape(b, HA * V_A, s), oct.reshape(b, HC * HD, s), ob, *consts)


def _t5_bucket(rel):
    nb = T5_BUCKETS // 2
    max_exact = nb // 2
    n = np.abs(rel)
    large = max_exact + (np.log(np.maximum(n, 1) / max_exact) / math.log(T5_MAX_DIST / max_exact)
                         * (nb - max_exact)).astype(np.int32)
    large = np.minimum(large, nb - 1)
    return (rel > 0).astype(np.int32) * nb + np.where(n < max_exact, n, large).astype(np.int32)


def _band_bias(rel_bias):
    span = 2 * BAND_R - 1
    rel = np.arange(-span, span + 1)
    band = np.where(np.abs(rel) <= BAND_R, 0.0, NEG_INF).astype(np.float32)
    out = []
    for g, (_, dil) in enumerate(DIL_PAIRS):
        tab = rel_bias[:, g * HPG_B:(g + 1) * HPG_B]
        by_rel = (tab[_t5_bucket(rel * dil)].astype(F32) + band[:, None]).T
        rows = [by_rel[:, BAND_R - 1 - q:BAND_R - 1 - q + 3 * BAND_R] for q in range(BAND_R)]
        out.append(jnp.stack(rows, axis=1).reshape(HPG_B * BAND_R, 3 * BAND_R))
    return jnp.stack(out, axis=0)


def _rope_tables(s):
    freqs = (ROPE_THETA ** (-np.arange(ROPE_HALF) / ROPE_HALF)).astype(np.float32)
    pos = jnp.arange(s, dtype=jnp.int32)
    rows = jnp.repeat(jnp.arange(s // GRID_W, dtype=jnp.int32), GRID_W)
    cols = pos % GRID_W
    tabs = []
    for p in (pos, rows, cols):
        ang = p.astype(F32)[:, None] * jnp.asarray(freqs)[None, :]
        tabs += [jnp.cos(ang).T, jnp.sin(ang).T]
    return jnp.stack(tabs, axis=0)


def _ffn_weights(w_in, w_out):
    assert w_out.shape[0] % FFN_CHUNK == 0 and w_in.shape[1] == 2 * w_out.shape[0]
    return w_in.astype(BF16), w_out.astype(BF16)


def _layer_weights(W, i):
    row = lambda v: v.reshape(1, -1).astype(F32)
    colv = lambda v: v.reshape(-1, 1).astype(F32)
    w_in = W["w_in"][i]
    w_a = w_in[:, :A_COLS]
    w_b = w_in[:, A_COLS:A_COLS + B_COLS]
    w_c = w_in[:, A_COLS + B_COLS:]
    head = np.arange(GROUP_B_COLS) // HD
    ones_blk = jnp.asarray((head[:, None] == head[None, :]).astype(np.float32) / HD, dtype=BF16)
    lw = dict(
        g_ffn1=row(W["g_ffn1"][i]), g_ffn2=row(W["g_ffn2"][i]), g_mix=row(W["g_mix"][i]), g_ple=row(W["g_ple"][i]),
        w_b=w_b.astype(BF16),
        w_act=jnp.concatenate([w_a, w_c], axis=1).T.astype(BF16),
        ones_blk=ones_blk,
        g_qb=row(jnp.tile(W["g_qb"][i], HPG_B)), g_kb=row(jnp.tile(W["g_kb"][i], HPG_B)),
        g_cq=colv(W["g_cq"][i]), g_ckv=colv(W["g_ckv"][i]),
        w_uqt=W["w_uq"][i].T.astype(BF16), w_ukvt=W["w_ukv"][i].T.astype(BF16),
        g_qa=colv(W["g_qa"][i]), g_ka=colv(W["g_ka"][i]), g_qc=colv(W["g_qc"][i]), g_kc=colv(W["g_kc"][i]),
        w_gate=W["w_gate"][i].astype(BF16), b_gate=row(W["b_gate"][i]),
        w_oa=W["w_oa"][i].astype(BF16), w_ob=W["w_ob"][i].astype(BF16), w_oc=W["w_oc"][i].astype(BF16),
        w_out=W["w_out"][i].astype(BF16),
        w_pg=W["w_pg"][i].astype(BF16), w_ple=W["w_ple"][i].astype(BF16),
    )
    lw["ffn1"] = _ffn_weights(W["w_ffn1_in"][i], W["w_ffn1_out"][i])
    lw["ffn2"] = _ffn_weights(W["w_ffn2_in"][i], W["w_ffn2_out"][i])
    return lw


def _trunk(x, p, layers, bias):
    b, s, d = x.shape
    tab = _rope_tables(s)
    for i, lw in enumerate(layers):
        x1 = _ffn(x.reshape(b * s, d), lw["g_ffn1"], *lw["ffn1"]).reshape(b, s, d)
        qkvb, qat, ka, vat, kna, qct, kc, vct, knc = _proj(x1, lw, tab)
        oat = _attention(qat, ka, vat, kna)
        oct = _attention(qct, kc, vct, knc)
        x2 = _merge(x1, oat, oct, _dilated(qkvb, bias), lw)
        x = _ffn(x2.reshape(b * s, d), lw["g_ffn2"], *lw["ffn2"],
                 ple_args=(p[i].reshape(b * s, PLE_DIM), lw["g_ple"], lw["w_pg"], lw["w_ple"])).reshape(b, s, d)
    return x


def kernel(x_prompt, x_sample, p_prompt, p_sample, g_ffn1, w_ffn1_in, w_ffn1_out, g_mix, w_in, g_cq, g_ckv, w_uq,
           w_ukv, g_qa, g_ka, g_qb, g_kb, rel_bias, g_qc, g_kc, w_gate, b_gate, w_oa, w_ob, w_oc, w_out, g_ffn2,
           w_ffn2_in, w_ffn2_out, g_ple, w_pg, w_ple):
    W = dict(g_ffn1=g_ffn1, w_ffn1_in=w_ffn1_in, w_ffn1_out=w_ffn1_out, g_mix=g_mix, w_in=w_in,
             g_cq=g_cq, g_ckv=g_ckv, w_uq=w_uq, w_ukv=w_ukv, g_qa=g_qa, g_ka=g_ka,
             g_qb=g_qb, g_kb=g_kb, g_qc=g_qc, g_kc=g_kc,
             w_gate=w_gate, b_gate=b_gate, w_oa=w_oa, w_ob=w_ob, w_oc=w_oc, w_out=w_out,
             g_ffn2=g_ffn2, w_ffn2_in=w_ffn2_in, w_ffn2_out=w_ffn2_out,
             g_ple=g_ple, w_pg=w_pg, w_ple=w_ple)
    layers = [_layer_weights(W, i) for i in range(g_mix.shape[0])]
    bias = _band_bias(rel_bias)
    return (_trunk(x_prompt, p_prompt, layers, bias), _trunk(x_sample, p_sample, layers, bias))
```

```python
import functools
import math

import numpy as np
import jax
import jax.numpy as jnp
from jax import lax
from jax.experimental import pallas as pl
from jax.experimental.pallas import tpu as pltpu

F32 = jnp.float32
BF16 = jnp.bfloat16

D_MODEL = 1024
GRID_W = 64
PLE_DIM = 256
D_FF = 2816
NORM_EPS = 1e-6
ROPE_THETA = 10000.0
NEG_INF = -1e30

HA = 8
Q_RANK = 256
KV_RANK = 128
NOPE_A = 64
ROPE_A = 32
V_A = 64
QK_A = NOPE_A + ROPE_A

HD = 64
DIL_PAIRS = ((128, 1), (512, 4), (2048, 16))
N_GROUPS_B = 3
HPG_B = 4
N_HEADS_B = N_GROUPS_B * HPG_B
T5_BUCKETS = 32
T5_MAX_DIST = 1024
BAND_R = 64
DIL_TILE = BAND_R * max(d for _, d in DIL_PAIRS)

HC = 8
KVC = 2
GC = HC // KVC

A_COLS = Q_RANK + KV_RANK + ROPE_A
B_COLS = 3 * N_HEADS_B * HD
C_COLS = (HC + 2 * KVC) * HD
GROUP_B_COLS = HPG_B * HD
N_Q_HEADS = HA + HC
N_KV_HEADS = HA + KVC
LANES = 128
LANE_HALVES = GROUP_B_COLS // LANES

ROPE_HALF = 16
QK_PAD = 128
LOG2E = math.log2(math.e)

VMEM_LIMIT_BYTES = 56 * 1024 * 1024
FFN_CHUNK = 256
SCORE_LOOKAHEAD = 2
DENOM_FLOOR = 2.0 ** -80


def _cparams(semantics):
    return pltpu.CompilerParams(dimension_semantics=semantics, vmem_limit_bytes=VMEM_LIMIT_BYTES)


def _const_spec(shape):
    nd = len(shape)
    return pl.BlockSpec(shape, lambda *_: (0,) * nd, pipeline_mode=pl.Buffered(1))


def _rms_rows(x, g):
    ms = jnp.mean(x * x, axis=-1, keepdims=True)
    return x * lax.rsqrt(ms + NORM_EPS) * g


def _rms_cols(x, g):
    ms = jnp.mean(x * x, axis=0, keepdims=True)
    return x * lax.rsqrt(ms + NORM_EPS) * g


def _rope_cols(x, cos, sin):
    x1, x2 = x[:ROPE_HALF], x[ROPE_HALF:]
    return x1 * cos - x2 * sin, x1 * sin + x2 * cos


def _ffn_kernel(*refs, n_chunks, ple):
    if ple:
        x_ref, g_ref, win_ref, wout_ref, pe_ref, gple_ref, wpg_ref, wple_ref, o_ref, acc_ref = refs
    else:
        x_ref, g_ref, win_ref, wout_ref, o_ref, acc_ref = refs
    x = x_ref[...]
    u = _rms_rows(x, g_ref[...]).astype(BF16)
    d_ff = wout_ref.shape[0]
    for c in range(n_chunks):
        lo, hi = c * FFN_CHUNK, (c + 1) * FFN_CHUNK
        a = jnp.dot(u, win_ref[:, lo:hi], preferred_element_type=F32)
        b = jnp.dot(u, win_ref[:, d_ff + lo:d_ff + hi], preferred_element_type=F32)
        h = (a * jax.nn.sigmoid(a) * b).astype(BF16)
        d = jnp.dot(h, wout_ref[lo:hi, :], preferred_element_type=F32)
        if c == 0:
            acc_ref[...] = d
        elif c < n_chunks - 1:
            acc_ref[...] += d
        else:
            y = x + 0.5 * (acc_ref[...] + d)
    if ple:
        gate = jax.nn.sigmoid(
            jnp.dot(_rms_rows(y, gple_ref[...]).astype(BF16), wpg_ref[...], preferred_element_type=F32))
        y = y + gate * jnp.dot(pe_ref[...].astype(BF16), wple_ref[...], preferred_element_type=F32)
    o_ref[...] = y


def _ffn(x2d, g, w_in_c, w_out_c, ple_args=None, tm=1024):
    n, d = x2d.shape
    n_chunks = w_out_c.shape[0] // FFN_CHUNK
    tm = min(tm, n)
    row = lambda i: (i, 0)
    in_specs = [pl.BlockSpec((tm, d), row), _const_spec(g.shape), _const_spec(w_in_c.shape), _const_spec(w_out_c.shape)]
    args = [x2d, g, w_in_c, w_out_c]
    if ple_args is not None:
        pe2d, g_ple, w_pg, w_ple = ple_args
        in_specs += [pl.BlockSpec((tm, pe2d.shape[1]), row), _const_spec(g_ple.shape), _const_spec(w_pg.shape),
                     _const_spec(w_ple.shape)]
        args += [pe2d, g_ple, w_pg, w_ple]
    return pl.pallas_call(
        functools.partial(_ffn_kernel, n_chunks=n_chunks, ple=ple_args is not None),
        grid=(n // tm,),
        in_specs=in_specs,
        out_specs=pl.BlockSpec((tm, d), row),
        out_shape=jax.ShapeDtypeStruct((n, d), F32),
        scratch_shapes=[pltpu.VMEM((tm, d), F32)],
        compiler_params=_cparams(("parallel",)),
        name="ffn_ple" if ple_args is not None else "ffn",
    )(*args)


_T_CQ = 0
_T_CKV = Q_RANK
_T_KR = Q_RANK + KV_RANK
_T_QC = A_COLS
_T_KC = A_COLS + HC * HD
_T_VC = A_COLS + (HC + KVC) * HD
_T_ROWS = A_COLS + C_COLS


def _proj_kernel(x_ref, gmix_ref, wb_ref, wact_ref, ones_ref, gqb_ref, gkb_ref, gcq_ref, gckv_ref, wuqt_ref,
                 wukvt_ref, gqa_ref, gka_ref, gqc_ref, gkc_ref, tab_ref,
                 qkvb_ref, qt_ref, kk_ref, vt_ref, kn_ref, *, scale_a, scale_c):
    ts = x_ref.shape[0]
    qat_ref, qct_ref = qt_ref.at[0:HA], qt_ref.at[HA:HA + HC]
    ka_ref, kc_ref = kk_ref.at[0:HA], kk_ref.at[HA:HA + KVC]
    vat_ref, vct_ref = vt_ref.at[0:HA], vt_ref.at[HA:HA + KVC]
    kna_ref, knc_ref = kn_ref.at[0:HA], kn_ref.at[HA:HA + KVC]

    def key_tail(rows):
        r = lax.broadcasted_iota(jnp.int32, (rows, ts), 0)
        return jnp.where(r == rows - 1, 1.0, 0.0).astype(F32)

    def key_norm(kt):
        return jnp.sqrt(jnp.sum(kt * kt, axis=0, keepdims=True))

    ub = _rms_rows(x_ref[...], gmix_ref[...]).astype(BF16)

    pb = jnp.dot(ub, wb_ref[...], preferred_element_type=F32)
    ones = ones_ref[...]
    for part, g_ref in ((0, gqb_ref), (1, gkb_ref), (2, None)):
        for c in range(N_GROUPS_B):
            blk = part * N_GROUPS_B + c
            xx = pb[:, blk * GROUP_B_COLS:(blk + 1) * GROUP_B_COLS]
            if g_ref is not None:
                ms = jnp.dot((xx * xx).astype(BF16), ones, preferred_element_type=F32)
                xx = xx * lax.rsqrt(ms + NORM_EPS) * g_ref[...]
            for half in range(LANE_HALVES):
                qkvb_ref[LANE_HALVES * blk + half] = xx[:, half * LANES:(half + 1) * LANES]

    pt = lax.dot_general(wact_ref[...], ub, (((1,), (1,)), ((), ())), preferred_element_type=F32)
    cos_p, sin_p, cos_r, sin_r, cos_c, sin_c = (tab_ref[i] for i in range(6))

    cq = _rms_cols(pt[_T_CQ:_T_CQ + Q_RANK], gcq_ref[...]).astype(BF16)
    qa = jnp.dot(wuqt_ref[...], cq, preferred_element_type=F32)
    ckv = _rms_cols(pt[_T_CKV:_T_CKV + KV_RANK], gckv_ref[...]).astype(BF16)
    kv = jnp.dot(wukvt_ref[...], ckv, preferred_element_type=F32)
    gqa = gqa_ref[...]
    gka = gka_ref[...]
    kr1, kr2 = _rope_cols(_rms_cols(pt[_T_KR:_T_KR + ROPE_A], gka[NOPE_A:]), cos_p, sin_p)
    k_tail = key_tail(QK_PAD - QK_A)
    kr_sq = jnp.sum(kr1 * kr1 + kr2 * kr2, axis=0, keepdims=True)
    for h in range(HA):
        q0 = h * QK_A
        qn = _rms_cols(qa[q0:q0 + NOPE_A], gqa[:NOPE_A])
        q1, q2 = _rope_cols(_rms_cols(qa[q0 + NOPE_A:q0 + QK_A], gqa[NOPE_A:]), cos_p, sin_p)
        qat_ref[h, 0:NOPE_A] = (qn * scale_a).astype(BF16)
        qat_ref[h, NOPE_A:NOPE_A + ROPE_HALF] = (q1 * scale_a).astype(BF16)
        qat_ref[h, NOPE_A + ROPE_HALF:QK_A] = (q2 * scale_a).astype(BF16)
        qat_ref[h, QK_A:QK_PAD] = jnp.zeros((QK_PAD - QK_A, ts), BF16)
        k0 = h * (NOPE_A + V_A)
        kn = _rms_cols(kv[k0:k0 + NOPE_A], gka[:NOPE_A])
        kt = jnp.concatenate([kn, kr1, kr2, k_tail], axis=0)
        ka_ref[h] = kt.T.astype(BF16)
        kna_ref[h] = jnp.sqrt(jnp.sum(kn * kn, axis=0, keepdims=True) + kr_sq)
        vat_ref[h] = kv[k0 + NOPE_A:k0 + NOPE_A + V_A].astype(BF16)

    def axial(t):
        a1, a2 = _rope_cols(t[:HD // 2], cos_r, sin_r)
        b1, b2 = _rope_cols(t[HD // 2:], cos_c, sin_c)
        return a1, a2, b1, b2

    gqc = gqc_ref[...]
    gkc = gkc_ref[...]
    for h in range(HC):
        parts = axial(_rms_cols(pt[_T_QC + h * HD:_T_QC + (h + 1) * HD], gqc) * scale_c)
        qct_ref[h, 0:HD] = jnp.concatenate(parts, axis=0).astype(BF16)
        qct_ref[h, HD:QK_PAD] = jnp.zeros((QK_PAD - HD, ts), BF16)
    c_tail = key_tail(QK_PAD - HD)
    for h in range(KVC):
        kh = jnp.concatenate(axial(_rms_cols(pt[_T_KC + h * HD:_T_KC + (h + 1) * HD], gkc)), axis=0)
        kt = jnp.concatenate([kh, c_tail], axis=0)
        kc_ref[h] = kt.T.astype(BF16)
        knc_ref[h] = key_norm(kh)
        vct_ref[h] = pt[_T_VC + h * HD:_T_VC + (h + 1) * HD].astype(BF16)


def _proj(x3d, lw, tab, ts=512):
    b, s, d = x3d.shape
    ts = min(ts, s)
    consts = [lw["g_mix"], lw["w_b"], lw["w_act"], lw["ones_blk"], lw["g_qb"], lw["g_kb"], lw["g_cq"], lw["g_ckv"],
              lw["w_uqt"], lw["w_ukvt"], lw["g_qa"], lw["g_ka"], lw["g_qc"], lw["g_kc"]]
    in_specs = ([pl.BlockSpec((None, ts, d), lambda bi, i: (bi, i, 0))] + [_const_spec(c.shape) for c in consts]
                + [pl.BlockSpec((6, ROPE_HALF, ts), lambda bi, i: (0, 0, i))])
    tok_major = lambda h: pl.BlockSpec((None, h, ts, QK_PAD), lambda bi, i: (bi, 0, i, 0))
    feat_major = lambda h, r: pl.BlockSpec((None, h, r, ts), lambda bi, i: (bi, 0, 0, i))
    out_specs = [pl.BlockSpec((None, B_COLS // LANES, ts, LANES), lambda bi, i: (bi, 0, i, 0)),
                 feat_major(N_Q_HEADS, QK_PAD), tok_major(N_KV_HEADS), feat_major(N_KV_HEADS, V_A),
                 feat_major(N_KV_HEADS, 1)]
    out_shape = [jax.ShapeDtypeStruct((b, B_COLS // LANES, s, LANES), F32),
                 jax.ShapeDtypeStruct((b, N_Q_HEADS, QK_PAD, s), BF16),
                 jax.ShapeDtypeStruct((b, N_KV_HEADS, s, QK_PAD), BF16),
                 jax.ShapeDtypeStruct((b, N_KV_HEADS, V_A, s), BF16), jax.ShapeDtypeStruct((b, N_KV_HEADS, 1, s), F32)]
    return pl.pallas_call(
        functools.partial(_proj_kernel, scale_a=QK_A ** -0.5 * LOG2E, scale_c=HD ** -0.5 * LOG2E),
        grid=(b, s // ts),
        in_specs=in_specs,
        out_specs=out_specs,
        out_shape=out_shape,
        compiler_params=_cparams(("parallel", "parallel")),
        name="proj",
    )(x3d, *consts, tab)


def _attn_kernel(kmax_ref, qt_ref, k_ref, vt_ref, ot_ref, acc_ref, den_ref, m_ref, *, tk, n_sub):
    s_len = k_ref.shape[0]
    qt = qt_ref[...]
    qf = qt.astype(F32)
    qn = jnp.sqrt(jnp.sum(qf * qf, axis=0, keepdims=True))
    bound = qn * kmax_ref[0:1, 0:1] * (1.0 + 2.0 ** -6) + 1.0
    row = lax.broadcasted_iota(jnp.int32, qt.shape, 0)
    qa = jnp.where(row == QK_PAD - 1, -bound, qf).astype(BF16)

    acc_ref[...] = jnp.zeros_like(acc_ref)
    den_ref[...] = jnp.zeros_like(den_ref)
    span = tk * n_sub

    def bounded(jj, carry):
        off = jj * span
        chunk = lambda u: pl.ds(pl.multiple_of(off + u * tk, tk), tk)
        score = lambda u: jnp.dot(k_ref[chunk(u), :], qa, preferred_element_type=F32)
        st = [score(u) for u in range(min(SCORE_LOOKAHEAD, n_sub))]
        tot = den = None
        for u in range(n_sub):
            if u + SCORE_LOOKAHEAD < n_sub:
                st.append(score(u + SCORE_LOOKAHEAD))
            p = jnp.exp2(st[u])
            pv = jnp.dot(vt_ref[:, chunk(u)], p.astype(BF16), preferred_element_type=F32)
            ps = jnp.sum(p, axis=0, keepdims=True)
            tot = pv if tot is None else tot + pv
            den = ps if den is None else den + ps
        acc_ref[...] += tot
        den_ref[...] += den
        return carry

    lax.fori_loop(0, s_len // span, bounded, 0)
    trusted = jnp.min(den_ref[...]) >= DENOM_FLOOR

    @pl.when(jnp.logical_not(trusted))
    def _():
        m_ref[...] = jnp.full_like(m_ref, NEG_INF)
        acc_ref[...] = jnp.zeros_like(acc_ref)
        den_ref[...] = jnp.zeros_like(den_ref)

        def running_max(j, carry):
            o = pl.multiple_of(j * tk, tk)
            st = jnp.dot(k_ref[pl.ds(o, tk), :], qt, preferred_element_type=F32)
            m_old = m_ref[...]
            m_new = jnp.maximum(m_old, jnp.max(st, axis=0, keepdims=True))
            alpha = jnp.exp2(m_old - m_new)
            p = jnp.exp2(st - m_new)
            acc_ref[...] = alpha * acc_ref[...] + jnp.dot(
                vt_ref[:, pl.ds(o, tk)], p.astype(BF16), preferred_element_type=F32)
            den_ref[...] = alpha * den_ref[...] + jnp.sum(p, axis=0, keepdims=True)
            m_ref[...] = m_new
            return carry

        lax.fori_loop(0, s_len // tk, running_max, 0)

    ot_ref[...] = (acc_ref[...] / den_ref[...]).astype(ot_ref.dtype)


def _attention(qt, k, vt, knorm, tq=512, tk=256, n_sub=64):
    b, hq, _, s = qt.shape
    hk = k.shape[1]
    tq = min(tq, s)
    tk = min(tk, s)
    n_sub = min(n_sub, s // tk)
    kmax = jnp.broadcast_to(jnp.max(knorm, axis=(2, 3))[:, :, None, None], (b, hk, 8, 128))
    kv_head = lambda h: jnp.where(h < HA, h, HA + (h - HA) // GC)
    return pl.pallas_call(
        functools.partial(_attn_kernel, tk=tk, n_sub=n_sub),
        grid=(b, hq, s // tq),
        in_specs=[pl.BlockSpec((None, None, 8, 128), lambda bi, h, i: (bi, kv_head(h), 0, 0)),
                  pl.BlockSpec((None, None, QK_PAD, tq), lambda bi, h, i: (bi, h, 0, i)),
                  pl.BlockSpec((None, None, s, QK_PAD), lambda bi, h, i: (bi, kv_head(h), 0, 0)),
                  pl.BlockSpec((None, None, V_A, s), lambda bi, h, i: (bi, kv_head(h), 0, 0))],
        out_specs=pl.BlockSpec((None, None, V_A, tq), lambda bi, h, i: (bi, h, 0, i)),
        out_shape=jax.ShapeDtypeStruct((b, hq, V_A, s), BF16),
        scratch_shapes=[pltpu.VMEM((V_A, tq), F32), pltpu.VMEM((1, tq), F32), pltpu.VMEM((1, tq), F32)],
        compiler_params=_cparams(("parallel", "parallel", "arbitrary")),
        name="attention",
    )(kmax, qt, k, vt)


def _dilated_kernel(*refs, n_t):
    n_in = 7 * N_GROUPS_B
    bias_ref, o_ref, kbuf, vbuf, os_ref, ls_ref = refs[n_in:]
    tile = o_ref.shape[0]
    rad = BAND_R
    ti = pl.program_id(1)
    col = lax.broadcasted_iota(jnp.int32, (1, 3 * rad), 1)
    edge_lo = jnp.where((col < rad) & (ti == 0), NEG_INF, 0.0)
    edge_hi = jnp.where((col >= 2 * rad) & (ti == n_t - 1), NEG_INF, 0.0)
    head_of_lane = lax.broadcasted_iota(jnp.int32, (1, GROUP_B_COLS), 1) // HD
    for g, (_, dil) in enumerate(DIL_PAIRS):
        q_ref, kp_ref, kc_ref, kn_ref, vp_ref, vc_ref, vn_ref = refs[7 * g:7 * g + 7]
        lblk = tile // dil
        n_blk = lblk // rad

        def class_rows(r, start, size):
            return pl.ds(start, size) if dil == 1 else pl.ds(r + dil * start, size, stride=dil)

        def rows(ref, r, start, size):
            sl = class_rows(r, start, size)
            return jnp.concatenate([ref[half, sl, :] for half in range(LANE_HALVES)], axis=1)

        for r in range(dil):
            for buf, p_ref, c_ref, n_ref in ((kbuf, kp_ref, kc_ref, kn_ref), (vbuf, vp_ref, vc_ref, vn_ref)):
                buf[0:rad] = rows(p_ref, r, 0, rad).astype(BF16)
                buf[rad:rad + lblk] = rows(c_ref, r, 0, lblk).astype(BF16)
                buf[rad + lblk:2 * rad + lblk] = rows(n_ref, r, 0, rad).astype(BF16)
            for c in range(n_blk):
                q = (rows(q_ref, r, c * rad, rad) * (HD ** -0.5)).astype(BF16)
                kw = kbuf[c * rad:(c + 3) * rad, :]
                vw = vbuf[c * rad:(c + 3) * rad, :]
                qs = jnp.concatenate([jnp.where(head_of_lane == h, q, jnp.zeros_like(q)) for h in range(HPG_B)],
                                     axis=0)
                s = lax.dot_general(qs, kw, (((1,), (1,)), ((), ())), preferred_element_type=F32)
                s = s + bias_ref[g]
                if c == 0:
                    s = s + edge_lo
                if c == n_blk - 1:
                    s = s + edge_hi
                m = jnp.max(s, axis=-1, keepdims=True)
                p = jnp.exp(s - m)
                l = jnp.sum(p, axis=-1, keepdims=True)
                ov = jnp.dot((p / l).astype(BF16), vw, preferred_element_type=F32)
                lse = m + jnp.log(l)
                o_acc = ov[(HPG_B - 1) * rad:]
                lse_acc = jnp.broadcast_to(lse[(HPG_B - 1) * rad:], (rad, GROUP_B_COLS))
                for h in range(HPG_B - 1):
                    sel = head_of_lane == h
                    o_acc = jnp.where(sel, ov[h * rad:(h + 1) * rad], o_acc)
                    lse_acc = jnp.where(sel, lse[h * rad:(h + 1) * rad], lse_acc)
                dst = class_rows(r, c * rad, rad)
                for half in range(LANE_HALVES):
                    os_ref[g, half, dst, :] = o_acc[:, half * LANES:(half + 1) * LANES]
                    ls_ref[g, half, dst, :] = lse_acc[:, half * LANES:(half + 1) * LANES]

    for half in range(LANE_HALVES):
        l0, l1, l2 = ls_ref[0, half], ls_ref[1, half], ls_ref[2, half]
        m = jnp.maximum(jnp.maximum(l0, l1), l2)
        e0, e1, e2 = jnp.exp(l0 - m), jnp.exp(l1 - m), jnp.exp(l2 - m)
        inv = 1.0 / (e0 + e1 + e2)
        o_ref[:, half * LANES:(half + 1) * LANES] = (
            (e0 * inv) * os_ref[0, half] + (e1 * inv) * os_ref[1, half] + (e2 * inv) * os_ref[2, half]
        ).astype(o_ref.dtype)


def _dilated(qkvb, bias):
    b, _, s, _ = qkvb.shape
    tile = DIL_TILE
    n_t = s // tile
    in_specs = []
    for g, (_, dil) in enumerate(DIL_PAIRS):
        halo = BAND_R * dil
        per_tile = tile // halo
        n_halo = s // halo

        def cur(part, g=g):
            return pl.BlockSpec((None, LANE_HALVES, tile, LANES), lambda bi, i: (bi, part * N_GROUPS_B + g, i, 0))

        def prev(part, g=g, halo=halo, per_tile=per_tile):
            return pl.BlockSpec((None, LANE_HALVES, halo, LANES),
                                lambda bi, i: (bi, part * N_GROUPS_B + g, jnp.maximum(i * per_tile - 1, 0), 0))

        def nxt(part, g=g, halo=halo, per_tile=per_tile, n_halo=n_halo):
            return pl.BlockSpec((None, LANE_HALVES, halo, LANES),
                                lambda bi, i: (bi, part * N_GROUPS_B + g, jnp.minimum((i + 1) * per_tile, n_halo - 1), 0))

        in_specs += [cur(0), prev(1), cur(1), nxt(1), prev(2), cur(2), nxt(2)]
    in_specs.append(pl.BlockSpec(bias.shape, lambda bi, i: (0, 0, 0)))
    return pl.pallas_call(
        functools.partial(_dilated_kernel, n_t=n_t),
        grid=(b, n_t),
        in_specs=in_specs,
        out_specs=pl.BlockSpec((None, tile, GROUP_B_COLS), lambda bi, i: (bi, i, 0)),
        out_shape=jax.ShapeDtypeStruct((b, s, GROUP_B_COLS), BF16),
        scratch_shapes=[pltpu.VMEM((tile + 2 * BAND_R, GROUP_B_COLS), BF16),
                        pltpu.VMEM((tile + 2 * BAND_R, GROUP_B_COLS), BF16),
                        pltpu.VMEM((N_GROUPS_B, LANE_HALVES, tile, LANES), F32),
                        pltpu.VMEM((N_GROUPS_B, LANE_HALVES, tile, LANES), F32)],
        compiler_params=_cparams(("parallel", "parallel")),
        name="dilated",
    )(*([qkvb] * (7 * N_GROUPS_B)), bias)


def _merge_kernel(x_ref, gmix_ref, oat_ref, oct_ref, ob_ref, wg_ref, bg_ref, woa_ref, wob_ref, woc_ref, wout_ref,
                  o_ref):
    x = x_ref[...]
    ub = _rms_rows(x, gmix_ref[...]).astype(BF16)
    d = x.shape[1]
    tn = (((0,), (0,)), ((), ()))
    ya = lax.dot_general(oat_ref[...], woa_ref[...], tn, preferred_element_type=F32)
    yb = jnp.dot(ob_ref[...], wob_ref[...], preferred_element_type=F32)
    yc = lax.dot_general(oct_ref[...], woc_ref[...], tn, preferred_element_type=F32)

    def gate(n):
        z = jnp.dot(ub, wg_ref[:, n * d:(n + 1) * d], preferred_element_type=F32) + bg_ref[:, n * d:(n + 1) * d]
        return jax.nn.sigmoid(z)

    merged = gate(0) * ya + gate(1) * yb + gate(2) * yc
    o_ref[...] = x + jnp.dot(merged.astype(BF16), wout_ref[...], preferred_element_type=F32)


def _merge(x3d, ot, ob, lw, ts=512):
    b, s, d = x3d.shape
    ts = min(ts, s)
    assert HA * V_A == HC * HD
    ot2 = ot.reshape(b, N_Q_HEADS * V_A, s)
    tok = lambda w: pl.BlockSpec((None, ts, w), lambda bi, i: (bi, i, 0))
    feat = lambda r, blk: pl.BlockSpec((None, r, ts), lambda bi, i: (bi, blk, i))
    consts = [lw["w_gate"], lw["b_gate"], lw["w_oa"], lw["w_ob"], lw["w_oc"], lw["w_out"]]
    return pl.pallas_call(
        _merge_kernel,
        grid=(b, s // ts),
        in_specs=[tok(d), _const_spec(lw["g_mix"].shape), feat(HA * V_A, 0), feat(HC * HD, 1), tok(GROUP_B_COLS)]
                 + [_const_spec(c.shape) for c in consts],
        out_specs=tok(d),
        out_shape=jax.ShapeDtypeStruct((b, s, d), F32),
        compiler_params=_cparams(("parallel", "parallel")),
        name="merge",
    )(x3d, lw["g_mix"], ot2, ot2, ob, *consts)


def _t5_bucket(rel):
    nb = T5_BUCKETS // 2
    max_exact = nb // 2
    n = np.abs(rel)
    large = max_exact + (np.log(np.maximum(n, 1) / max_exact) / math.log(T5_MAX_DIST / max_exact)
                         * (nb - max_exact)).astype(np.int32)
    large = np.minimum(large, nb - 1)
    return (rel > 0).astype(np.int32) * nb + np.where(n < max_exact, n, large).astype(np.int32)


def _band_bias(rel_bias):
    span = 2 * BAND_R - 1
    rel = np.arange(-span, span + 1)
    band = np.where(np.abs(rel) <= BAND_R, 0.0, NEG_INF).astype(np.float32)
    out = []
    for g, (_, dil) in enumerate(DIL_PAIRS):
        tab = rel_bias[:, g * HPG_B:(g + 1) * HPG_B]
        by_rel = (tab[_t5_bucket(rel * dil)].astype(F32) + band[:, None]).T
        rows = [by_rel[:, BAND_R - 1 - q:BAND_R - 1 - q + 3 * BAND_R] for q in range(BAND_R)]
        out.append(jnp.stack(rows, axis=1).reshape(HPG_B * BAND_R, 3 * BAND_R))
    return jnp.stack(out, axis=0)


def _rope_tables(s):
    freqs = (ROPE_THETA ** (-np.arange(ROPE_HALF) / ROPE_HALF)).astype(np.float32)
    pos = jnp.arange(s, dtype=jnp.int32)
    rows = jnp.repeat(jnp.arange(s // GRID_W, dtype=jnp.int32), GRID_W)
    cols = pos % GRID_W
    tabs = []
    for p in (pos, rows, cols):
        ang = p.astype(F32)[:, None] * jnp.asarray(freqs)[None, :]
        tabs += [jnp.cos(ang).T, jnp.sin(ang).T]
    return jnp.stack(tabs, axis=0)


def _ffn_weights(w_in, w_out):
    assert w_out.shape[0] % FFN_CHUNK == 0 and w_in.shape[1] == 2 * w_out.shape[0]
    return w_in.astype(BF16), w_out.astype(BF16)


def _layer_weights(W, i):
    row = lambda v: v.reshape(1, -1).astype(F32)
    colv = lambda v: v.reshape(-1, 1).astype(F32)
    w_in = W["w_in"][i]
    w_a = w_in[:, :A_COLS]
    w_b = w_in[:, A_COLS:A_COLS + B_COLS]
    w_c = w_in[:, A_COLS + B_COLS:]
    head = np.arange(GROUP_B_COLS) // HD
    ones_blk = jnp.asarray((head[:, None] == head[None, :]).astype(np.float32) / HD, dtype=BF16)
    lw = dict(
        g_ffn1=row(W["g_ffn1"][i]), g_ffn2=row(W["g_ffn2"][i]), g_mix=row(W["g_mix"][i]), g_ple=row(W["g_ple"][i]),
        w_b=w_b.astype(BF16),
        w_act=jnp.concatenate([w_a, w_c], axis=1).T.astype(BF16),
        ones_blk=ones_blk,
        g_qb=row(jnp.tile(W["g_qb"][i], HPG_B)), g_kb=row(jnp.tile(W["g_kb"][i], HPG_B)),
        g_cq=colv(W["g_cq"][i]), g_ckv=colv(W["g_ckv"][i]),
        w_uqt=W["w_uq"][i].T.astype(BF16), w_ukvt=W["w_ukv"][i].T.astype(BF16),
        g_qa=colv(W["g_qa"][i]), g_ka=colv(W["g_ka"][i]), g_qc=colv(W["g_qc"][i]), g_kc=colv(W["g_kc"][i]),
        w_gate=W["w_gate"][i].astype(BF16), b_gate=row(W["b_gate"][i]),
        w_oa=W["w_oa"][i].astype(BF16), w_ob=W["w_ob"][i].astype(BF16), w_oc=W["w_oc"][i].astype(BF16),
        w_out=W["w_out"][i].astype(BF16),
        w_pg=W["w_pg"][i].astype(BF16), w_ple=W["w_ple"][i].astype(BF16),
    )
    lw["ffn1"] = _ffn_weights(W["w_ffn1_in"][i], W["w_ffn1_out"][i])
    lw["ffn2"] = _ffn_weights(W["w_ffn2_in"][i], W["w_ffn2_out"][i])
    return lw


def _trunk(x, p, layers, bias):
    b, s, d = x.shape
    tab = _rope_tables(s)
    for i, lw in enumerate(layers):
        x1 = _ffn(x.reshape(b * s, d), lw["g_ffn1"], *lw["ffn1"]).reshape(b, s, d)
        qkvb, qt, kk, vt, kn = _proj(x1, lw, tab)
        x2 = _merge(x1, _attention(qt, kk, vt, kn), _dilated(qkvb, bias), lw)
        x = _ffn(x2.reshape(b * s, d), lw["g_ffn2"], *lw["ffn2"],
                 ple_args=(p[i].reshape(b * s, PLE_DIM), lw["g_ple"], lw["w_pg"], lw["w_ple"])).reshape(b, s, d)
    return x


def kernel(x_prompt, x_sample, p_prompt, p_sample, g_ffn1, w_ffn1_in, w_ffn1_out, g_mix, w_in, g_cq, g_ckv, w_uq,
           w_ukv, g_qa, g_ka, g_qb, g_kb, rel_bias, g_qc, g_kc, w_gate, b_gate, w_oa, w_ob, w_oc, w_out, g_ffn2,
           w_ffn2_in, w_ffn2_out, g_ple, w_pg, w_ple):
    W = dict(g_ffn1=g_ffn1, w_ffn1_in=w_ffn1_in, w_ffn1_out=w_ffn1_out, g_mix=g_mix, w_in=w_in,
             g_cq=g_cq, g_ckv=g_ckv, w_uq=w_uq, w_ukv=w_ukv, g_qa=g_qa, g_ka=g_ka,
             g_qb=g_qb, g_kb=g_kb, g_qc=g_qc, g_kc=g_kc,
             w_gate=w_gate, b_gate=b_gate, w_oa=w_oa, w_ob=w_ob, w_oc=w_oc, w_out=w_out,
             g_ffn2=g_ffn2, w_ffn2_in=w_ffn2_in, w_ffn2_out=w_ffn2_out,
             g_ple=g_ple, w_pg=w_pg, w_ple=w_ple)
    layers = [_layer_weights(W, i) for i in range(g_mix.shape[0])]
    bias = _band_bias(rel_bias)
    return (_trunk(x_prompt, p_prompt, layers, bias), _trunk(x_sample, p_sample, layers, bias))
```
